```python
import jax, jax.numpy as jnp
from jax import lax
import numpy as np

D_MODEL = 4096
BATCH = 1
SEQ = 8192
DEPTH = 4
DEC_BATCH = 2
DEC_SEQ = 8192
PAST_LEN = 128

HEAD_DIM = 128
D_MIX = D_MODEL
D_GROUP = D_MIX // 4
RET_HEADS = D_GROUP // HEAD_DIM
ATT_HEADS = D_GROUP // HEAD_DIM
ATT_KV_HEADS = max(ATT_HEADS // 4, 1)
LRU_WIDTH = D_GROUP
LRU_BLOCKS = D_GROUP // HEAD_DIM
LRU_BLOCK_W = LRU_WIDTH // LRU_BLOCKS
DN_HEADS = D_GROUP // HEAD_DIM
D_FF = ((8 * D_MODEL // 3 + 255) // 256) * 256
GRID_W = 64
ROPE_THETA = 10000.0
AXIS_PAIRS = HEAD_DIM // 4
RET_CHUNK = 128
ATT_BLOCK = 128
DN_CHUNK = 64
SHORT_CONV = 4
SHORT_CONV_LEFT = 2
FFN_CONV = 3
FFN_CONV_LEFT = 1
LRU_C = 8.0
EPS = 1e-6

SPLIT_SIZES = (D_GROUP, D_GROUP, D_GROUP, D_GROUP,
               ATT_HEADS * HEAD_DIM, ATT_KV_HEADS * HEAD_DIM, ATT_KV_HEADS * HEAD_DIM,
               LRU_WIDTH, LRU_WIDTH,
               3 * D_GROUP, D_GROUP, 2 * DN_HEADS, 2 * DN_HEADS)
IN_COLS = sum(SPLIT_SIZES)
SPLIT_POINTS = tuple(int(p) for p in np.cumsum(SPLIT_SIZES)[:-1])

kernel_name = 'hybrid_bidir_encoder_retention_gqa_rglru_deltanet'

F32 = jnp.float32


def rms_norm(x, g):
    xf = x.astype(F32)
    y = xf * lax.rsqrt(jnp.mean(xf * xf, axis=-1, keepdims=True) + EPS)
    return (y * g.astype(F32)).astype(x.dtype)


def l2_norm(x):
    xf = x.astype(F32)
    return xf * lax.rsqrt(jnp.sum(xf * xf, axis=-1, keepdims=True) + EPS)


def dw_conv(x, w, left):
    k = w.shape[0]
    return lax.conv_general_dilated(x, w[:, None, :].astype(x.dtype), (1,), [(left, k - 1 - left)],
                                    dimension_numbers=('NWC', 'WIO', 'NWC'),
                                    feature_group_count=x.shape[-1])


def axial_rope(n):
    rows = n // GRID_W
    row = jnp.repeat(jnp.arange(rows, dtype=F32), GRID_W)
    col = jnp.tile(jnp.arange(GRID_W, dtype=F32), rows)
    inv = ROPE_THETA ** (-jnp.arange(AXIS_PAIRS, dtype=F32) / AXIS_PAIRS)
    ang = jnp.stack([row[:, None] * inv, col[:, None] * inv], axis=1)
    return jnp.cos(ang), jnp.sin(ang)


def apply_rope(x, cos, sin):
    b, n, h, d = x.shape
    xr = x.astype(F32).reshape(b, n, h, 2, 2, AXIS_PAIRS)
    x1, x2 = xr[..., 0, :], xr[..., 1, :]
    c = cos[None, :, None]
    s = sin[None, :, None]
    return jnp.stack([x1 * c - x2 * s, x2 * c + x1 * s], axis=-2).reshape(b, n, h, d)


def retention_scan(q, k, v, log_gamma, inclusive):
    b, h, n, d = q.shape
    nc = n // RET_CHUNK
    idx = jnp.arange(RET_CHUNK, dtype=F32)
    diff = idx[:, None] - idx[None, :]
    keep = (diff >= 0) if inclusive else (diff > 0)
    lg = log_gamma.astype(F32)[:, None]
    intra_decay = jnp.where(keep, jnp.exp(lg[:, :, None] * jnp.maximum(diff, 0.0)), 0.0)
    q_decay = jnp.exp(lg * (idx + 1.0))[..., None]
    k_decay = jnp.exp(lg * (RET_CHUNK - 1.0 - idx))[..., None]
    chunk_decay = jnp.exp(lg * RET_CHUNK)[:, :, None]

    def to_chunks(t):
        return jnp.moveaxis(t.reshape(b, h, nc, RET_CHUNK, d), 2, 0)

    def step(state, inp):
        qc, kc, vc = inp
        scores = jnp.einsum('bhid,bhjd->bhij', qc, kc) * intra_decay
        out = (jnp.einsum('bhij,bhjv->bhiv', scores, vc)
               + jnp.einsum('bhid,bhdv->bhiv', qc * q_decay, state))
        state = chunk_decay * state + jnp.einsum('bhjd,bhjv->bhdv', kc * k_decay, vc)
        return state, out

    s0 = jnp.zeros((b, h, d, d), F32)
    _, out = lax.scan(step, s0, (to_chunks(q), to_chunks(k), to_chunks(v)))
    return jnp.moveaxis(out, 0, 2).reshape(b, h, n, d)


def bidirectional_retention(q, k, v, decay_logits):
    lg = jax.nn.log_sigmoid(decay_logits.astype(F32))
    fwd = retention_scan(q, k, v, lg[0], True)
    flip = lambda t: jnp.flip(t, axis=2)
    bwd = flip(retention_scan(flip(q), flip(k), flip(v), lg[1], False))
    return fwd + bwd


def block_attention(q, k, v):
    b, n, hq, d = q.shape
    hkv = k.shape[2]
    grp = hq // hkv
    nb = n // ATT_BLOCK
    qb = q.reshape(b, nb, ATT_BLOCK, hkv, grp, d).transpose(1, 0, 3, 4, 2, 5)
    kt = k.transpose(0, 2, 1, 3)
    vt = v.transpose(0, 2, 1, 3)
    scale = d ** -0.5

    def attend(qblk):
        s = jnp.einsum('bhgqd,bhkd->bhgqk', qblk, kt).astype(F32) * scale
        p = jax.nn.softmax(s, axis=-1)
        return jnp.einsum('bhgqk,bhkd->bhgqd', p.astype(vt.dtype), vt)

    o = lax.map(attend, qb)
    return o.transpose(1, 0, 4, 2, 3, 5).reshape(b, n, hq * d)


def rg_lru(x, gate_w, gate_b, lam, reverse):
    b, n, w = x.shape
    xb = x.reshape(b, n, LRU_BLOCKS, LRU_BLOCK_W)
    gates = jax.nn.sigmoid(jnp.einsum('bnki,gkij->gbnkj', xb, gate_w.astype(F32))
                           + gate_b.astype(F32)[:, None, None]).reshape(2, b, n, w)
    r, i = gates[0], gates[1]
    log_a = -LRU_C * r * jax.nn.softplus(-lam.astype(F32))
    a = jnp.exp(log_a)
    u = jnp.sqrt(-jnp.expm1(2.0 * log_a)) * (i * x)

    def combine(left, right):
        a1, b1 = left
        a2, b2 = right
        return a1 * a2, a2 * b1 + b2

    _, hs = lax.associative_scan(combine, (a, u), axis=1, reverse=reverse)
    return hs


def gated_delta_chunked(q, k, v, beta, g):
    b, h, n, dk = q.shape
    dv = v.shape[-1]
    c = DN_CHUNK
    nc = n // c
    q, k, v = [t.reshape(b, h, nc, c, t.shape[-1]) for t in (q, k, v)]
    beta = beta.reshape(b, h, nc, c)
    g = jnp.cumsum(g.reshape(b, h, nc, c), axis=-1)
    incl = jnp.tril(jnp.ones((c, c), bool))
    strict = jnp.tril(jnp.ones((c, c), bool), -1)
    decay = jnp.exp(jnp.where(incl, g[..., :, None] - g[..., None, :], -jnp.inf))
    k_beta = k * beta[..., None]
    m = jnp.where(strict, jnp.einsum('bhcid,bhcjd->bhcij', k_beta, k) * decay, 0.0)
    eye = jnp.eye(c, dtype=F32)
    t_inv = lax.linalg.triangular_solve(eye + m, jnp.broadcast_to(eye, m.shape),
                                        left_side=True, lower=True, unit_diagonal=True)
    u0 = t_inv @ (v * beta[..., None])
    w = t_inv @ (k_beta * jnp.exp(g)[..., None])
    a_intra = jnp.einsum('bhcid,bhcjd->bhcij', q, k) * decay
    q_g = q * jnp.exp(g)[..., None]
    k_tail = k * jnp.exp(g[..., -1:] - g)[..., None]
    chunk_decay = jnp.exp(g[..., -1])[..., None, None]

    def step(state, inp):
        u0c, wc, ac, qc, kc, dc = inp
        u = u0c - wc @ state
        out = qc @ state + ac @ u
        state = state * dc + jnp.swapaxes(kc, -1, -2) @ u
        return state, out

    xs = tuple(jnp.moveaxis(t, 2, 0) for t in (u0, w, a_intra, q_g, k_tail, chunk_decay))
    _, out = lax.scan(step, jnp.zeros((b, h, dk, dv), F32), xs)
    return jnp.moveaxis(out, 0, 2).reshape(b, h, n, dv)


def delta_mixer(qkv, z, beta_raw, a_raw, conv_w, a_log, dt_bias, out_g):
    b, n, _ = qkv.shape
    qkv = jax.nn.silu(dw_conv(qkv, conv_w, SHORT_CONV_LEFT)).astype(F32)
    heads = lambda t: t.reshape(b, n, DN_HEADS, HEAD_DIM).transpose(0, 2, 1, 3)
    q, k, v = [heads(t) for t in jnp.split(qkv, 3, axis=-1)]
    q = l2_norm(q) * HEAD_DIM ** -0.5
    k = l2_norm(k)
    beta = jax.nn.sigmoid(beta_raw.astype(F32)).reshape(b, n, 2, DN_HEADS).transpose(2, 0, 3, 1)
    a_in = a_raw.astype(F32).reshape(b, n, 2, DN_HEADS).transpose(2, 0, 3, 1)
    g = -jnp.exp(a_log.astype(F32))[:, None, :, None] * jax.nn.softplus(a_in + dt_bias.astype(F32)[:, None, :, None])
    flip = lambda t: jnp.flip(t, axis=2)
    fwd = gated_delta_chunked(q, k, v, beta[0], g[0])
    bwd = flip(gated_delta_chunked(flip(q), flip(k), flip(v), flip(beta[1]), flip(g[1])))
    o = (fwd + bwd).transpose(0, 2, 1, 3)
    o = rms_norm(o, out_g) * jax.nn.silu(z.astype(F32)).reshape(b, n, DN_HEADS, HEAD_DIM)
    return o.reshape(b, n, D_GROUP)


def conv_ffn(h, w_gate, w_up, conv_w, conv_b, w_down):
    gate = dw_conv(h @ w_gate, conv_w, FFN_CONV_LEFT) + conv_b
    return (jax.nn.gelu(gate, approximate=True) * (h @ w_up)) @ w_down


def encoder_layer(x, c_act, cos, sin, ada_w, ada_b, norm_g, w_in, ret_decay, ret_out_g, attn_qk_g, attn_out_g,
                  lru_conv_w, lru_conv_b, lru_gate_w, lru_gate_b, lru_lambda, lru_out_g,
                  dn_conv_w, dn_a_log, dn_dt_bias, dn_out_g, w_out,
                  ffn_w_gate, ffn_w_up, ffn_conv_w, ffn_conv_b, ffn_w_down):
    b, n, _ = x.shape
    mod = (c_act @ ada_w + ada_b)[:, None, :]
    shift1, scale1, gate1, shift2, scale2, gate2 = jnp.split(mod, 6, axis=-1)

    h = rms_norm(x, norm_g[0]) * (1 + scale1) + shift1
    (rq, rk, rv, rg, aq, ak, av, lx, lgate, dqkv, dz, dbeta, da) = jnp.split(h @ w_in, SPLIT_POINTS, axis=-1)
    heads = lambda t: t.reshape(b, n, -1, HEAD_DIM)

    rq = apply_rope(heads(rq), cos, sin).transpose(0, 2, 1, 3)
    rk = (apply_rope(heads(rk), cos, sin) * HEAD_DIM ** -0.5).transpose(0, 2, 1, 3)
    rv = heads(rv).astype(F32).transpose(0, 2, 1, 3)
    ret = bidirectional_retention(rq, rk, rv, ret_decay).transpose(0, 2, 1, 3)
    ret = (rms_norm(ret, ret_out_g.reshape(RET_HEADS, HEAD_DIM))
           * jax.nn.silu(heads(rg).astype(F32))).reshape(b, n, D_GROUP)

    aq = apply_rope(rms_norm(heads(aq), attn_qk_g[0]), cos, sin).astype(x.dtype)
    ak = apply_rope(rms_norm(heads(ak), attn_qk_g[1]), cos, sin).astype(x.dtype)
    att = block_attention(aq, ak, heads(av))
    att = rms_norm(heads(att).astype(F32), attn_out_g.reshape(ATT_HEADS, HEAD_DIM)).reshape(b, n, D_GROUP)

    lx = (dw_conv(lx, lru_conv_w, SHORT_CONV_LEFT) + lru_conv_b).astype(F32)
    rec = (rg_lru(lx, lru_gate_w[0], lru_gate_b[0], lru_lambda[0], False)
           + rg_lru(lx, lru_gate_w[1], lru_gate_b[1], lru_lambda[1], True))
    rec = rec * jax.nn.gelu(lgate.astype(F32), approximate=True)
    rec = rms_norm(rec.reshape(b, n, LRU_BLOCKS, LRU_BLOCK_W),
                   lru_out_g.reshape(LRU_BLOCKS, LRU_BLOCK_W)).reshape(b, n, LRU_WIDTH)

    dn = delta_mixer(dqkv, dz, dbeta, da, dn_conv_w, dn_a_log, dn_dt_bias, dn_out_g)

    mixed = jnp.concatenate([ret, att, rec, dn], axis=-1).astype(x.dtype) @ w_out
    x = x + gate1 * rms_norm(mixed, norm_g[1])

    h = rms_norm(x, norm_g[2]) * (1 + scale2) + shift2
    f = conv_ffn(h, ffn_w_gate, ffn_w_up, ffn_conv_w, ffn_conv_b, ffn_w_down)
    x = x + gate2 * rms_norm(f, norm_g[3])
    return x


def encode(x, c, params):
    cos, sin = axial_rope(x.shape[1])
    c_act = jax.nn.silu(c)
    for layer in range(DEPTH):
        layer_params = tuple(p[layer] for p in params)
        x = encoder_layer(x, c_act, cos, sin, *layer_params)
    return x


def setup_inputs(seed: int = 0) -> dict:
    key = jax.random.key(seed)
    ks = jax.random.split(key, 32)
    nrm = lambda k, shape, s: jax.random.normal(k, shape, F32) * s
    d = D_MODEL
    ret_base = jnp.log(2.0 ** (5.0 + jnp.arange(RET_HEADS, dtype=F32)) - 1.0)
    a0 = jax.random.uniform(ks[14], (DEPTH, 2, LRU_WIDTH), F32, 0.9, 0.999)
    dt = jnp.exp(jax.random.uniform(ks[18], (DEPTH, 2, DN_HEADS), F32, jnp.log(0.001), jnp.log(0.1)))
    return {
        'x_prompt': nrm(ks[0], (BATCH, SEQ, d), 1.0),
        'x_sample': nrm(ks[1], (DEC_BATCH, DEC_SEQ, d), 1.0),
        'c_prompt': nrm(ks[2], (BATCH, d), 1.0),
        'c_sample': nrm(ks[3], (DEC_BATCH, d), 1.0),
        'ada_w': nrm(ks[4], (DEPTH, d, 6 * d), 0.3 * d ** -0.5),
        'ada_b': nrm(ks[5], (DEPTH, 6 * d), 0.02),
        'norm_g': 1.0 + nrm(ks[6], (DEPTH, 4, d), 0.02),
        'w_in': nrm(ks[7], (DEPTH, d, IN_COLS), d ** -0.5),
        'ret_decay': ret_base + nrm(ks[8], (DEPTH, 2, RET_HEADS), 0.01),
        'ret_out_g': 1.0 + nrm(ks[9], (DEPTH, D_GROUP), 0.02),
        'attn_qk_g': 1.0 + nrm(ks[10], (DEPTH, 2, HEAD_DIM), 0.02),
        'attn_out_g': 1.0 + nrm(ks[11], (DEPTH, D_GROUP), 0.02),
        'lru_conv_w': nrm(ks[12], (DEPTH, SHORT_CONV, LRU_WIDTH), SHORT_CONV ** -0.5),
        'lru_conv_b': nrm(ks[13], (DEPTH, LRU_WIDTH), 0.02),
        'lru_gate_w': nrm(ks[15], (DEPTH, 2, 2, LRU_BLOCKS, LRU_BLOCK_W, LRU_BLOCK_W), LRU_BLOCK_W ** -0.5),
        'lru_gate_b': nrm(ks[16], (DEPTH, 2, 2, LRU_BLOCKS, LRU_BLOCK_W), 0.02),
        'lru_lambda': jnp.log(a0) - jnp.log1p(-a0),
        'lru_out_g': 1.0 + nrm(ks[17], (DEPTH, LRU_WIDTH), 0.02),
        'dn_conv_w': nrm(ks[19], (DEPTH, SHORT_CONV, 3 * D_GROUP), SHORT_CONV ** -0.5),
        'dn_a_log': jnp.log(jax.random.uniform(ks[20], (DEPTH, 2, DN_HEADS), F32, 1.0, 16.0)),
        'dn_dt_bias': dt + jnp.log(-jnp.expm1(-dt)),
        'dn_out_g': 1.0 + nrm(ks[21], (DEPTH, HEAD_DIM), 0.02),
        'w_out': nrm(ks[22], (DEPTH, D_MIX, d), D_MIX ** -0.5),
        'ffn_w_gate': nrm(ks[23], (DEPTH, d, D_FF), d ** -0.5),
        'ffn_w_up': nrm(ks[24], (DEPTH, d, D_FF), d ** -0.5),
        'ffn_conv_w': nrm(ks[25], (DEPTH, FFN_CONV, D_FF), FFN_CONV ** -0.5),
        'ffn_conv_b': nrm(ks[26], (DEPTH, D_FF), 0.02),
        'ffn_w_down': nrm(ks[27], (DEPTH, D_FF, d), D_FF ** -0.5),
    }


def reference(x_prompt, x_sample, c_prompt, c_sample, ada_w, ada_b, norm_g, w_in, ret_decay, ret_out_g,
              attn_qk_g, attn_out_g, lru_conv_w, lru_conv_b, lru_gate_w, lru_gate_b, lru_lambda, lru_out_g,
              dn_conv_w, dn_a_log, dn_dt_bias, dn_out_g, w_out, ffn_w_gate, ffn_w_up, ffn_conv_w, ffn_conv_b,
              ffn_w_down):
    params = (ada_w, ada_b, norm_g, w_in, ret_decay, ret_out_g, attn_qk_g, attn_out_g,
              lru_conv_w, lru_conv_b, lru_gate_w, lru_gate_b, lru_lambda, lru_out_g,
              dn_conv_w, dn_a_log, dn_dt_bias, dn_out_g, w_out,
              ffn_w_gate, ffn_w_up, ffn_conv_w, ffn_conv_b, ffn_w_down)
    y_prompt = encode(x_prompt, c_prompt, params)
    y_sample = encode(x_sample, c_sample, params)
    return (y_prompt, y_sample)
```

```python
import functools
import math

import jax
import jax.numpy as jnp
from jax import lax
from jax.experimental import pallas as pl
from jax.experimental.pallas import tpu as pltpu

F32 = jnp.float32
BF16 = jnp.bfloat16

HEAD_DIM = 128
GRID_W = 64
ROPE_THETA = 10000.0
EPS = 1e-6
LRU_C = 8.0
DN_CHUNK = 64
V7X_VMEM_LIMIT = 56 * 1024 * 1024
NEG_BIG = -1e30


def _cp(sem, vmem=V7X_VMEM_LIMIT):
    return pltpu.CompilerParams(dimension_semantics=sem, vmem_limit_bytes=vmem)


def _pick(n, cands):
    for c in cands:
        if n % c == 0:
            return c
    return n


def _bdot(a, b):
    return jnp.dot(a.astype(BF16), b.astype(BF16), preferred_element_type=F32)


def _bdot_nt(a, b):
    return lax.dot_general(a.astype(BF16), b.astype(BF16), (((1,), (1,)), ((), ())), preferred_element_type=F32)


def _bdot_tn(a, b):
    return lax.dot_general(a.astype(BF16), b.astype(BF16), (((0,), (0,)), ((), ())), preferred_element_type=F32)


def _rms(x, g):
    return x * lax.rsqrt(jnp.mean(x * x, axis=-1, keepdims=True) + EPS) * g


def _rope(x, c, s1, s2):
    return x * c + pltpu.roll(x, 96, 1) * s1 + pltpu.roll(x, 32, 1) * s2


def _neg_expm1(z):
    series = -z * (1.0 + z * 0.5 * (1.0 + z * (1.0 / 3.0) * (1.0 + z * 0.25 * (1.0 + z * 0.2))))
    return jnp.where(z > -0.02, series, 1.0 - jnp.exp(z))


def _ada_kernel(c_ref, w_ref, b_ref, o_ref):
    c = c_ref[...]
    o_ref[...] = _bdot(c * jax.nn.sigmoid(c), w_ref[...]) + b_ref[...]


def _ada(c8, ada_w, ada_b):
    L, D, M = ada_w.shape
    tn = _pick(M, (512, 256, 128))
    return pl.pallas_call(
        _ada_kernel,
        grid=(L, M // tn),
        in_specs=[pl.BlockSpec((8, D), lambda l, j: (0, 0)),
                  pl.BlockSpec((None, D, tn), lambda l, j: (l, 0, j)),
                  pl.BlockSpec((None, 1, tn), lambda l, j: (l, 0, j))],
        out_specs=pl.BlockSpec((None, 8, tn), lambda l, j: (l, 0, j)),
        out_shape=jax.ShapeDtypeStruct((L, 8, M), F32),
        compiler_params=_cp(("parallel", "parallel")),
        name="ada_mod",
    )(c8, ada_w, ada_b.reshape(L, 1, M))


def _resnorm_kernel(*refs, has_f, has_h):
    it = iter(refs)
    x_ref = next(it)
    if has_f:
        f_ref, ga_ref, gate_ref = next(it), next(it), next(it)
    if has_h:
        gb_ref, scale_ref, shift_ref = next(it), next(it), next(it)
    if has_f:
        xo_ref = next(it)
    if has_h:
        h_ref = next(it)
    x = x_ref[...]
    if has_f:
        x = x + gate_ref[...] * _rms(f_ref[...], ga_ref[...])
        xo_ref[...] = x
    if has_h:
        h_ref[...] = (_rms(x, gb_ref[...]) * (1.0 + scale_ref[...]) + shift_ref[...]).astype(h_ref.dtype)


def _resnorm(x, B, N, *, f=None, ga=None, gate=None, gb=None, scale=None, shift=None):
    T, D = x.shape
    tr = _pick(N, (256, 128, 64, 8))
    nt = N // tr
    has_f, has_h = f is not None, gb is not None
    row = pl.BlockSpec((tr, D), lambda b, t: (b * nt + t, 0))
    gain = pl.BlockSpec((1, D), lambda b, t: (0, 0))

    def modarg(mp):
        mod, part = mp
        return mod.reshape(mod.shape[0], 1, mod.shape[1]), pl.BlockSpec((None, 1, D), lambda b, t: (b, 0, part))

    args, specs = [x], [row]
    if has_f:
        m, sp = modarg(gate)
        args += [f, ga.reshape(1, D), m]
        specs += [row, gain, sp]
    if has_h:
        (m1, sp1), (m2, sp2) = modarg(scale), modarg(shift)
        args += [gb.reshape(1, D), m1, m2]
        specs += [gain, sp1, sp2]
    out_shape, out_specs = [], []
    if has_f:
        out_shape.append(jax.ShapeDtypeStruct((T, D), F32))
        out_specs.append(row)
    if has_h:
        out_shape.append(jax.ShapeDtypeStruct((T, D), BF16))
        out_specs.append(row)
    return pl.pallas_call(
        functools.partial(_resnorm_kernel, has_f=has_f, has_h=has_h),
        grid=(B, nt), in_specs=specs, out_specs=out_specs, out_shape=out_shape,
        compiler_params=_cp(("parallel", "parallel")),
        name="resnorm",
    )(*args)


def _mm_kernel(a_ref, w_ref, o_ref, *acc, nk):
    if nk == 1:
        o_ref[...] = jnp.dot(a_ref[...], w_ref[...], preferred_element_type=F32).astype(o_ref.dtype)
        return
    acc_ref, = acc
    k = pl.program_id(2)
    part = jnp.dot(a_ref[...], w_ref[...], preferred_element_type=F32)

    @pl.when(k == 0)
    def _():
        acc_ref[...] = part

    @pl.when(jnp.logical_and(k > 0, k < nk - 1))
    def _():
        acc_ref[...] += part

    @pl.when(k == nk - 1)
    def _():
        o_ref[...] = (acc_ref[...] + part).astype(o_ref.dtype)


def _split_k(K):
    nk = 1
    while K // nk > 6144 or K % nk or (K // nk) % 128:
        nk += 1
    return nk


def _mm(a, w, out_dtype, name):
    M, K = a.shape
    N = w.shape[1]
    nk = _split_k(K)
    tk = K // nk
    tm = _pick(M, (1024, 512, 256, 128))
    tn = _pick(N, (1024, 512, 256, 128)) if nk == 1 else _pick(N, (512, 256, 128))
    return pl.pallas_call(
        functools.partial(_mm_kernel, nk=nk),
        grid=(M // tm, N // tn, nk),
        in_specs=[pl.BlockSpec((tm, tk), lambda i, j, k: (i, k)),
                  pl.BlockSpec((tk, tn), lambda i, j, k: (k, j))],
        out_specs=pl.BlockSpec((tm, tn), lambda i, j, k: (i, j)),
        out_shape=jax.ShapeDtypeStruct((M, N), out_dtype),
        scratch_shapes=[pltpu.VMEM((tm, tn), F32)] if nk > 1 else [],
        compiler_params=_cp(("parallel", "parallel", "arbitrary")),
        name=name,
    )(a, w)


def _mm4_kernel(a0, a1, a2, a3, w_ref, o_ref, *, dg):
    acc = jnp.dot(a0[...], w_ref[0:dg, :], preferred_element_type=F32)
    acc += jnp.dot(a1[...], w_ref[dg:2 * dg, :], preferred_element_type=F32)
    acc += jnp.dot(a2[...], w_ref[2 * dg:3 * dg, :], preferred_element_type=F32)
    acc += jnp.dot(a3[...], w_ref[3 * dg:4 * dg, :], preferred_element_type=F32)
    o_ref[...] = acc


def _mm_mixed(parts, w):
    M, dg = parts[0].shape
    N = w.shape[1]
    tm = _pick(M, (1024, 512, 256, 128))
    tn = _pick(N, (1024, 512, 256, 128))
    a_spec = pl.BlockSpec((tm, dg), lambda i, j: (i, 0))
    return pl.pallas_call(
        functools.partial(_mm4_kernel, dg=dg),
        grid=(M // tm, N // tn),
        in_specs=[a_spec] * 4 + [pl.BlockSpec((4 * dg, tn), lambda i, j: (0, j))],
        out_specs=pl.BlockSpec((tm, tn), lambda i, j: (i, j)),
        out_shape=jax.ShapeDtypeStruct((M, N), F32),
        compiler_params=_cp(("parallel", "parallel")),
        name="mm_out",
    )(*parts, w)


FFN_HALO = 16


def _ffn_gu_kernel(h_ref, hp_ref, hn_ref, wg_ref, wu_ref, cw_ref, cb_ref, o_ref, hext_ref, g_ref, *, tm, tiles_per_seq):
    i, j = pl.program_id(0), pl.program_id(1)

    @pl.when(j == 0)
    def _():
        hext_ref[0:FFN_HALO, :] = hp_ref[...]
        hext_ref[FFN_HALO:FFN_HALO + tm, :] = h_ref[...]
        hext_ref[FFN_HALO + tm:, :] = hn_ref[...]

    g_ref[...] = jnp.dot(hext_ref[...], wg_ref[...], preferred_element_type=F32)
    up = jnp.dot(h_ref[...], wu_ref[...], preferred_element_type=F32)
    row = lax.broadcasted_iota(jnp.int32, (tm, 1), 0)
    first = (i % tiles_per_seq) == 0
    last = (i % tiles_per_seq) == tiles_per_seq - 1
    g_prev = jnp.where(jnp.logical_and(first, row == 0), 0.0, g_ref[FFN_HALO - 1:FFN_HALO - 1 + tm, :])
    g_next = jnp.where(jnp.logical_and(last, row == tm - 1), 0.0, g_ref[FFN_HALO + 1:FFN_HALO + 1 + tm, :])
    g_cur = g_ref[FFN_HALO:FFN_HALO + tm, :]
    cw = cw_ref[...]
    gate = cw[0:1, :] * g_prev + cw[1:2, :] * g_cur + cw[2:3, :] * g_next + cb_ref[...]
    o_ref[...] = (jax.nn.gelu(gate, approximate=True) * up).astype(o_ref.dtype)


def _ffn_gu(h, wg, wu, cw, cb, N):
    T, D = h.shape
    F = wg.shape[1]
    tm = _pick(N, (512, 256, 128))
    tn = _pick(F, (512, 256, 128))
    tps = N // tm
    hb = tm // FFN_HALO
    nhb = T // FFN_HALO
    return pl.pallas_call(
        functools.partial(_ffn_gu_kernel, tm=tm, tiles_per_seq=tps),
        grid=(T // tm, F // tn),
        in_specs=[pl.BlockSpec((tm, D), lambda i, j: (i, 0)),
                  pl.BlockSpec((FFN_HALO, D), lambda i, j: (jnp.maximum(i * hb - 1, 0), 0)),
                  pl.BlockSpec((FFN_HALO, D), lambda i, j: (jnp.minimum((i + 1) * hb, nhb - 1), 0)),
                  pl.BlockSpec((D, tn), lambda i, j: (0, j)),
                  pl.BlockSpec((D, tn), lambda i, j: (0, j)),
                  pl.BlockSpec((3, tn), lambda i, j: (0, j)),
                  pl.BlockSpec((1, tn), lambda i, j: (0, j))],
        out_specs=pl.BlockSpec((tm, tn), lambda i, j: (i, j)),
        out_shape=jax.ShapeDtypeStruct((T, F), BF16),
        scratch_shapes=[pltpu.VMEM((tm + 2 * FFN_HALO, D), BF16), pltpu.VMEM((tm + 2 * FFN_HALO, tn), F32)],
        compiler_params=_cp(("parallel", "arbitrary")),
        name="ffn_gate_up",
    )(h, h, h, wg, wu, cw, cb.reshape(1, F))


def _ret_decays(dec_ref, d, h):
    x = dec_ref[d, h]
    return jax.nn.log_sigmoid(x)


def _ret_bwd_kernel(q_ref, k_ref, v_ref, c_ref, s1_ref, s2_ref, dec_ref, ob_ref, st_ref, *, H, C):
    @pl.when(pl.program_id(1) == 0)
    def _():
        st_ref[...] = jnp.zeros_like(st_ref)

    cs, s1, s2 = c_ref[...], s1_ref[...], s2_ref[...]
    idx = lax.broadcasted_iota(jnp.int32, (C, 1), 0).astype(F32)
    for h in range(H):
        sl = slice(h * HEAD_DIM, (h + 1) * HEAD_DIM)
        lg = _ret_decays(dec_ref, 1, h)
        q = _rope(q_ref[:, sl], cs, s1, s2)
        k = _rope(k_ref[:, sl], cs, s1, s2) * (HEAD_DIM ** -0.5)
        state = st_ref[h]
        ob_ref[:, sl] = _bdot(q * jnp.exp(lg * (C - idx)), state)
        st_ref[h] = jnp.exp(lg * C) * state + _bdot_tn(k * jnp.exp(lg * idx), v_ref[:, sl])


def _ret_fwd_kernel(q_ref, k_ref, v_ref, g_ref, ob_ref, c_ref, s1_ref, s2_ref, dec_ref, og_ref, o_ref, st_ref, *, H, C):
    @pl.when(pl.program_id(1) == 0)
    def _():
        st_ref[...] = jnp.zeros_like(st_ref)

    cs, s1, s2 = c_ref[...], s1_ref[...], s2_ref[...]
    idx = lax.broadcasted_iota(jnp.int32, (C, 1), 0).astype(F32)
    diff = (lax.broadcasted_iota(jnp.int32, (C, C), 0) - lax.broadcasted_iota(jnp.int32, (C, C), 1)).astype(F32)
    for h in range(H):
        sl = slice(h * HEAD_DIM, (h + 1) * HEAD_DIM)
        lgf = _ret_decays(dec_ref, 0, h)
        lgb = _ret_decays(dec_ref, 1, h)
        q = _rope(q_ref[:, sl], cs, s1, s2)
        k = _rope(k_ref[:, sl], cs, s1, s2) * (HEAD_DIM ** -0.5)
        v = v_ref[:, sl]
        decay = jnp.exp(jnp.where(diff >= 0, lgf * diff, -lgb * diff))
        scores = _bdot_nt(q, k) * decay
        state = st_ref[h]
        o = _bdot(scores, v) + _bdot(q * jnp.exp(lgf * (idx + 1.0)), state) + ob_ref[:, sl]
        st_ref[h] = jnp.exp(lgf * C) * state + _bdot_tn(k * jnp.exp(lgf * (C - 1.0 - idx)), v)
        g = g_ref[:, sl]
        o_ref[:, sl] = (_rms(o, og_ref[:, sl]) * (g * jax.nn.sigmoid(g))).astype(o_ref.dtype)


def _retention(P, tabs, ret_decay, ret_out_g, B, N, Dg):
    T = P.shape[0]
    H = Dg // HEAD_DIM
    C = _pick(N, (256, 128, 64))
    nc = N // C
    dec = ret_decay.reshape(2, H, 1, 1)
    dec_spec = pl.BlockSpec((2, H, 1, 1), lambda b, c: (0, 0, 0, 0))
    st = pltpu.VMEM((H, HEAD_DIM, HEAD_DIM), F32)

    def seg(s, rev):
        if rev:
            return pl.BlockSpec((C, Dg), lambda b, c: (b * nc + nc - 1 - c, s))
        return pl.BlockSpec((C, Dg), lambda b, c: (b * nc + c, s))

    def tab(rev):
        if rev:
            return pl.BlockSpec((C, HEAD_DIM), lambda b, c: (nc - 1 - c, 0))
        return pl.BlockSpec((C, HEAD_DIM), lambda b, c: (c, 0))

    ob = pl.pallas_call(
        functools.partial(_ret_bwd_kernel, H=H, C=C),
        grid=(B, nc),
        in_specs=[seg(0, True), seg(1, True), seg(2, True), tab(True), tab(True), tab(True), dec_spec],
        out_specs=seg(0, True),
        out_shape=jax.ShapeDtypeStruct((T, Dg), F32),
        scratch_shapes=[st],
        compiler_params=_cp(("parallel", "arbitrary")),
        name="ret_bwd",
    )(P, P, P, *tabs, dec)
    return pl.pallas_call(
        functools.partial(_ret_fwd_kernel, H=H, C=C),
        grid=(B, nc),
        in_specs=[seg(0, False), seg(1, False), seg(2, False), seg(3, False), seg(0, False),
                  tab(False), tab(False), tab(False), dec_spec, pl.BlockSpec((1, Dg), lambda b, c: (0, 0))],
        out_specs=seg(0, False),
        out_shape=jax.ShapeDtypeStruct((T, Dg), BF16),
        scratch_shapes=[st],
        compiler_params=_cp(("parallel", "arbitrary")),
        name="ret_fwd",
    )(P, P, P, P, ob, *tabs, dec, ret_out_g.reshape(1, Dg))


def _att_prep_kernel(q_ref, k_ref, v_ref, c_ref, s1_ref, s2_ref, g_ref, qo_ref, ko_ref, vo_ref, *, H, Hkv):
    cs, s1, s2 = c_ref[...], s1_ref[...], s2_ref[...]
    gq, gk = g_ref[0:1, :], g_ref[1:2, :]
    for h in range(H):
        sl = slice(h * HEAD_DIM, (h + 1) * HEAD_DIM)
        qo_ref[:, sl] = (_rope(_rms(q_ref[:, sl], gq), cs, s1, s2) * (HEAD_DIM ** -0.5)).astype(qo_ref.dtype)
    for h in range(Hkv):
        sl = slice(h * HEAD_DIM, (h + 1) * HEAD_DIM)
        ko_ref[:, sl] = _rope(_rms(k_ref[:, sl], gk), cs, s1, s2).astype(ko_ref.dtype)
    vo_ref[...] = v_ref[...].astype(vo_ref.dtype)


def _flash_kernel(q_ref, k_ref, v_ref, og_ref, o_ref, m_ref, l_ref, acc_ref, *, grp, nk):
    ki = pl.program_id(3)

    @pl.when(ki == 0)
    def _():
        m_ref[...] = jnp.full_like(m_ref, -jnp.inf)
        l_ref[...] = jnp.zeros_like(l_ref)
        acc_ref[...] = jnp.zeros_like(acc_ref)

    k = k_ref[...]
    v = v_ref[...]
    for g in range(grp):
        sl = slice(g * HEAD_DIM, (g + 1) * HEAD_DIM)
        s = lax.dot_general(q_ref[:, sl], k, (((1,), (1,)), ((), ())), preferred_element_type=F32)
        m_old = m_ref[g]
        m_new = jnp.maximum(m_old, jnp.max(s, axis=-1, keepdims=True))
        alpha = jnp.exp(m_old - m_new)
        p = jnp.exp(s - m_new)
        l_ref[g] = alpha * l_ref[g] + jnp.sum(p, axis=-1, keepdims=True)
        acc_ref[g] = alpha * acc_ref[g] + jnp.dot(p.astype(BF16), v, preferred_element_type=F32)
        m_ref[g] = m_new

    @pl.when(ki == nk - 1)
    def _():
        for g in range(grp):
            sl = slice(g * HEAD_DIM, (g + 1) * HEAD_DIM)
            o = acc_ref[g] / l_ref[g]
            o_ref[:, sl] = _rms(o, og_ref[:, sl]).astype(o_ref.dtype)


def _attention(P, tabs, attn_qk_g, attn_out_g, B, N, Dg):
    T = P.shape[0]
    H = Dg // HEAD_DIM
    Hkv = max(H // 4, 1)
    grp = H // Hkv
    Dkv = Hkv * HEAD_DIM
    tr = _pick(N, (512, 256, 128))
    nrt = N // tr
    tab = pl.BlockSpec((tr, HEAD_DIM), lambda r: (r % nrt, 0))
    qa, ka, va = pl.pallas_call(
        functools.partial(_att_prep_kernel, H=H, Hkv=Hkv),
        grid=(T // tr,),
        in_specs=[pl.BlockSpec((tr, Dg), lambda r: (r, 4)),
                  pl.BlockSpec((tr, Dkv), lambda r: (r, 11 * grp)),
                  pl.BlockSpec((tr, Dkv), lambda r: (r, 11 * grp + 1)),
                  tab, tab, tab,
                  pl.BlockSpec((2, HEAD_DIM), lambda r: (0, 0))],
        out_specs=[pl.BlockSpec((tr, Dg), lambda r: (r, 0)),
                   pl.BlockSpec((tr, Dkv), lambda r: (r, 0)),
                   pl.BlockSpec((tr, Dkv), lambda r: (r, 0))],
        out_shape=[jax.ShapeDtypeStruct((T, Dg), BF16), jax.ShapeDtypeStruct((T, Dkv), BF16),
                   jax.ShapeDtypeStruct((T, Dkv), BF16)],
        compiler_params=_cp(("parallel",)),
        name="att_prep",
    )(P, P, P, *tabs, attn_qk_g)

    tq = _pick(N, (512, 256, 128))
    tk = _pick(N, (1024, 512, 256, 128))
    nq, nk = N // tq, N // tk
    gw = grp * HEAD_DIM
    return pl.pallas_call(
        functools.partial(_flash_kernel, grp=grp, nk=nk),
        grid=(B, Hkv, nq, nk),
        in_specs=[pl.BlockSpec((tq, gw), lambda b, h, qi, ki: (b * nq + qi, h)),
                  pl.BlockSpec((tk, HEAD_DIM), lambda b, h, qi, ki: (b * nk + ki, h)),
                  pl.BlockSpec((tk, HEAD_DIM), lambda b, h, qi, ki: (b * nk + ki, h)),
                  pl.BlockSpec((1, gw), lambda b, h, qi, ki: (0, h))],
        out_specs=pl.BlockSpec((tq, gw), lambda b, h, qi, ki: (b * nq + qi, h)),
        out_shape=jax.ShapeDtypeStruct((T, Dg), BF16),
        scratch_shapes=[pltpu.VMEM((grp, tq, 1), F32), pltpu.VMEM((grp, tq, 1), F32),
                        pltpu.VMEM((grp, tq, HEAD_DIM), F32)],
        compiler_params=_cp(("parallel", "parallel", "parallel", "arbitrary")),
        name="flash_attn",
    )(qa, ka, va, attn_out_g.reshape(1, Dg))


def _conv4(cur, prev8, next8, w, n):
    win = jnp.concatenate([prev8, cur, next8], axis=0)
    return (w[0:1, :] * win[6:6 + n] + w[1:2, :] * win[7:7 + n]
            + w[2:3, :] * win[8:8 + n] + w[3:4, :] * win[9:9 + n])


def _scan_steps(a, u, n, reverse):
    row = lax.broadcasted_iota(jnp.int32, (n, 1), 0)
    d = 1
    while d < n:
        if d < 8:
            if reverse:
                keep = row < n - d
                a_s, u_s = pltpu.roll(a, n - d, 0), pltpu.roll(u, n - d, 0)
            else:
                keep = row >= d
                a_s, u_s = pltpu.roll(a, d, 0), pltpu.roll(u, d, 0)
            a_s = jnp.where(keep, a_s, 1.0)
            u_s = jnp.where(keep, u_s, 0.0)
        else:
            ones = jnp.ones((d, a.shape[1]), F32)
            zeros = jnp.zeros((d, a.shape[1]), F32)
            if reverse:
                a_s = jnp.concatenate([a[d:], ones], axis=0)
                u_s = jnp.concatenate([u[d:], zeros], axis=0)
            else:
                a_s = jnp.concatenate([ones, a[:n - d]], axis=0)
                u_s = jnp.concatenate([zeros, u[:n - d]], axis=0)
        u = a * u_s + u
        a = a * a_s
        d *= 2
    return a, u


def _lru_kernel(x_ref, gt_ref, cw_ref, cb_ref, gw_ref, gb_ref, lam_ref, og_ref, o_ref, hf_ref, *, N, Tc):
    nc = N // Tc
    cw = cw_ref[...]
    cb = cb_ref[...]
    W = x_ref.shape[1]

    def conv_chunk(c):
        t0 = pl.multiple_of(c * Tc, Tc)
        cur = x_ref[pl.ds(t0, Tc), :]
        p0 = pl.multiple_of(jnp.maximum(t0 - 8, 0), 8)
        n0 = pl.multiple_of(jnp.minimum(t0 + Tc, N - 8), 8)
        prev8 = jnp.where(c > 0, x_ref[pl.ds(p0, 8), :], 0.0)
        next8 = jnp.where(c < nc - 1, x_ref[pl.ds(n0, 8), :], 0.0)
        return _conv4(cur, prev8, next8, cw, Tc) + cb

    def gates(x, d):
        z = jnp.dot(x.astype(BF16), gw_ref[d], preferred_element_type=F32) + gb_ref[d]
        r = jax.nn.sigmoid(z[:, :W])
        i = jax.nn.sigmoid(z[:, W:])
        lam = lam_ref[d:d + 1, :]
        log_a = -LRU_C * r * jax.nn.softplus(-lam)
        a = jnp.exp(log_a)
        u = jnp.sqrt(_neg_expm1(2.0 * log_a)) * (i * x)
        return a, u

    def fwd(c, carry):
        x = conv_chunk(c)
        a, u = gates(x, 0)
        a, u = _scan_steps(a, u, Tc, False)
        h = u + a * carry
        hf_ref[pl.ds(pl.multiple_of(c * Tc, Tc), Tc), :] = h
        return h[Tc - 1:Tc, :]

    lax.fori_loop(0, nc, fwd, jnp.zeros((1, W), F32))

    def bwd(ci, carry):
        c = nc - 1 - ci
        t0 = pl.multiple_of(c * Tc, Tc)
        x = conv_chunk(c)
        a, u = gates(x, 1)
        a, u = _scan_steps(a, u, Tc, True)
        h = u + a * carry
        rec = (hf_ref[pl.ds(t0, Tc), :] + h) * jax.nn.gelu(gt_ref[pl.ds(t0, Tc), :], approximate=True)
        o_ref[pl.ds(t0, Tc), :] = _rms(rec, og_ref[...]).astype(o_ref.dtype)
        return h[0:1, :]

    lax.fori_loop(0, nc, bwd, jnp.zeros((1, W), F32))


def _rg_lru(P, conv_w, conv_b, gate_w, gate_b, lam, out_g, B, N, Dg):
    T = P.shape[0]
    W = HEAD_DIM
    K = Dg // W
    Tc = _pick(N, (256, 128, 64))
    gw = jnp.concatenate([gate_w[:, 0], gate_w[:, 1]], axis=-1).astype(BF16)
    gb = jnp.concatenate([gate_b[:, 0], gate_b[:, 1]], axis=-1).reshape(2, K, 1, 2 * W)
    base = Dg // W
    return pl.pallas_call(
        functools.partial(_lru_kernel, N=N, Tc=Tc),
        grid=(B, K),
        in_specs=[pl.BlockSpec((N, W), lambda b, k: (b, 5 * base + k)),
                  pl.BlockSpec((N, W), lambda b, k: (b, 6 * base + k)),
                  pl.BlockSpec((4, W), lambda b, k: (0, k)),
                  pl.BlockSpec((1, W), lambda b, k: (0, k)),
                  pl.BlockSpec((2, None, W, 2 * W), lambda b, k: (0, k, 0, 0)),
                  pl.BlockSpec((2, None, 1, 2 * W), lambda b, k: (0, k, 0, 0)),
                  pl.BlockSpec((2, W), lambda b, k: (0, k)),
                  pl.BlockSpec((1, W), lambda b, k: (0, k))],
        out_specs=pl.BlockSpec((N, W), lambda b, k: (b, k)),
        out_shape=jax.ShapeDtypeStruct((T, Dg), BF16),
        scratch_shapes=[pltpu.VMEM((N, W), F32)],
        compiler_params=_cp(("parallel", "parallel")),
        name="rg_lru",
    )(P, P, conv_w, conv_b.reshape(1, Dg), gw, gb, lam, out_g.reshape(1, Dg))


def _dn_prep_kernel(x_ref, xp_ref, xn_ref, w_ref, o_ref, *, n, nt, H, mode):
    t = pl.program_id(1)
    prev8 = jnp.where(t > 0, xp_ref[...], 0.0)
    next8 = jnp.where(t < nt - 1, xn_ref[...], 0.0)
    y = _conv4(x_ref[...], prev8, next8, w_ref[...], n)
    y = y * jax.nn.sigmoid(y)
    if mode == "v":
        o_ref[...] = y
        return
    scale = HEAD_DIM ** -0.5 if mode == "q" else 1.0
    for h in range(H):
        sl = slice(h * HEAD_DIM, (h + 1) * HEAD_DIM)
        yh = y[:, sl]
        o_ref[:, sl] = yh * (lax.rsqrt(jnp.sum(yh * yh, axis=-1, keepdims=True) + EPS) * scale)


def _dn_gates_kernel(g_ref, al_ref, dt_ref, o_ref, *, H):
    raw = g_ref[...]
    n = raw.shape[0]
    lane = lax.broadcasted_iota(jnp.int32, (n, HEAD_DIM), 1)
    for h in range(H):
        cols = []
        for d in range(2):
            beta = jax.nn.sigmoid(raw[:, d * H + h:d * H + h + 1])
            a_in = raw[:, (2 + d) * H + h:(2 + d) * H + h + 1]
            g = -jnp.exp(jnp.full((1, 1), al_ref[d, h], F32)) * jax.nn.softplus(a_in + dt_ref[d, h])
            cols += [beta, g]
        bf, gf, bb, gb = cols
        slab = jnp.where(lane == 0, bf, jnp.where(lane == 1, bb, jnp.where(lane == 2, gf, jnp.where(lane == 3, gb, 0.0))))
        o_ref[:, h * HEAD_DIM:(h + 1) * HEAD_DIM] = slab


def _split2(x):
    hi = x.astype(BF16)
    lo = (x - hi.astype(F32)).astype(BF16)
    return hi, lo


def _dot3(a, b):
    ah, al = _split2(a)
    bh, bl = _split2(b)
    return (jnp.dot(ah, bh, preferred_element_type=F32) + jnp.dot(ah, bl, preferred_element_type=F32)
            + jnp.dot(al, bh, preferred_element_type=F32))


def _dn_chunk_prep(q, k, v, beta, g, reverse):
    C = q.shape[0]
    ri = lax.broadcasted_iota(jnp.int32, (C, C), 0)
    ci = lax.broadcasted_iota(jnp.int32, (C, C), 1)
    incl = (ri <= ci) if reverse else (ri >= ci)
    strict = (ri < ci) if reverse else (ri > ci)
    gb = jnp.broadcast_to(g, (C, HEAD_DIM))
    gcol = gb
    row = lax.broadcasted_iota(jnp.int32, (C, 1), 0)
    d = 1
    while d < C:
        if reverse:
            sh = pltpu.roll(gcol, C - d, 0)
            gcol = gcol + jnp.where(row < C - d, sh, 0.0)
        else:
            sh = pltpu.roll(gcol, d, 0)
            gcol = gcol + jnp.where(row >= d, sh, 0.0)
        d *= 2
    vis = (ri >= ci) if reverse else (ri <= ci)
    grow = jnp.sum(jnp.where(vis, gb[:, :C], 0.0), axis=0, keepdims=True)
    decay = jnp.exp(jnp.where(incl, gcol[:, :C] - grow, NEG_BIG))
    kb = k * beta
    akk = _bdot_nt(kb, k)
    m = jnp.where(strict, akk * decay, 0.0)
    eye = (ri == ci).astype(F32)
    p = -m
    tinv = eye + p
    steps = int(math.log2(C)) - 1
    for _ in range(steps):
        p = _dot3(p, p)
        tinv = tinv + _dot3(tinv, p)
    eg = jnp.exp(gcol)
    u0 = _bdot(tinv, v * beta)
    w = _bdot(tinv, kb * eg)
    aqk = _bdot_nt(q, k) * decay
    qg = q * eg
    glast = gcol[0:1, :] if reverse else gcol[C - 1:C, :]
    ktail = k * jnp.exp(glast - gcol)
    cdecay = jnp.exp(glast)
    return u0, w, aqk, qg, ktail, cdecay


def _dn_kernel(qf_ref, kf_ref, vf_ref, gf_ref, qb_ref, kb_ref, vb_ref, gb_ref, of_ref, ob_ref, sf_ref, sb_ref, *, Tb, C):
    @pl.when(pl.program_id(2) == 0)
    def _():
        sf_ref[...] = jnp.zeros_like(sf_ref)
        sb_ref[...] = jnp.zeros_like(sb_ref)

    ncb = Tb // C

    def one(q_ref, k_ref, v_ref, g_ref, o_ref, s_ref, c, reverse):
        r0 = pl.multiple_of(c * C, C)
        rows = pl.ds(r0, C)
        gate = g_ref[rows, :]
        if reverse:
            beta, g = gate[:, 1:2], gate[:, 3:4]
        else:
            beta, g = gate[:, 0:1], gate[:, 2:3]
        u0, w, aqk, qg, ktail, cdecay = _dn_chunk_prep(q_ref[rows, :], k_ref[rows, :], v_ref[rows, :], beta, g, reverse)
        state = s_ref[...]
        u = u0 - _bdot(w, state)
        o_ref[rows, :] = _bdot(qg, state) + _bdot(aqk, u)
        s_ref[...] = state * cdecay + _bdot_tn(ktail, u)

    def body(c, carry):
        one(qf_ref, kf_ref, vf_ref, gf_ref, of_ref, sf_ref, c, False)
        one(qb_ref, kb_ref, vb_ref, gb_ref, ob_ref, sb_ref, ncb - 1 - c, True)
        return carry

    lax.fori_loop(0, ncb, body, 0)


def _dn_out_kernel(of_ref, ob_ref, z_ref, g_ref, o_ref, *, H):
    g = g_ref[...]
    for h in range(H):
        sl = slice(h * HEAD_DIM, (h + 1) * HEAD_DIM)
        z = z_ref[:, sl]
        o_ref[:, sl] = (_rms(of_ref[:, sl] + ob_ref[:, sl], g) * (z * jax.nn.sigmoid(z))).astype(o_ref.dtype)


def _delta_net(P, conv_w, a_log, dt_bias, out_g, B, N, Dg):
    T = P.shape[0]
    H = Dg // HEAD_DIM
    tp = _pick(N, (256, 128, 64))
    nt = N // tp
    hb = tp // 8
    nhb = T // 8

    def prep(seg, mode):
        return pl.pallas_call(
            functools.partial(_dn_prep_kernel, n=tp, nt=nt, H=H, mode=mode),
            grid=(B, nt),
            in_specs=[pl.BlockSpec((tp, Dg), lambda b, t: (b * nt + t, 7 + seg)),
                      pl.BlockSpec((8, Dg), lambda b, t: (jnp.maximum((b * nt + t) * hb - 1, 0), 7 + seg)),
                      pl.BlockSpec((8, Dg), lambda b, t: (jnp.minimum((b * nt + t + 1) * hb, nhb - 1), 7 + seg)),
                      pl.BlockSpec((4, Dg), lambda b, t: (0, seg))],
            out_specs=pl.BlockSpec((tp, Dg), lambda b, t: (b * nt + t, 0)),
            out_shape=jax.ShapeDtypeStruct((T, Dg), F32),
            compiler_params=_cp(("parallel", "parallel")),
            name="dn_prep_" + mode,
        )(P, P, P, conv_w)

    qn, kn, vn = prep(0, "q"), prep(1, "k"), prep(2, "v")

    tg = _pick(T, (512, 256, 128))
    gcol = (11 * Dg + 2 * max(H // 4, 1) * HEAD_DIM) // HEAD_DIM
    smem = pl.BlockSpec(memory_space=pltpu.SMEM)
    gates = pl.pallas_call(
        functools.partial(_dn_gates_kernel, H=H),
        grid=(T // tg,),
        in_specs=[pl.BlockSpec((tg, HEAD_DIM), lambda r: (r, gcol)), smem, smem],
        out_specs=pl.BlockSpec((tg, Dg), lambda r: (r, 0)),
        out_shape=jax.ShapeDtypeStruct((T, Dg), F32),
        compiler_params=_cp(("parallel",)),
        name="dn_gates",
    )(P, a_log, dt_bias)

    Tb = _pick(N, (512, 256, 128, 64))
    ntb = N // Tb
    fspec = pl.BlockSpec((Tb, HEAD_DIM), lambda b, h, t: (b * ntb + t, h))
    bspec = pl.BlockSpec((Tb, HEAD_DIM), lambda b, h, t: (b * ntb + ntb - 1 - t, h))
    st = pltpu.VMEM((HEAD_DIM, HEAD_DIM), F32)
    of, ob = pl.pallas_call(
        functools.partial(_dn_kernel, Tb=Tb, C=DN_CHUNK),
        grid=(B, H, ntb),
        in_specs=[fspec] * 4 + [bspec] * 4,
        out_specs=[fspec, bspec],
        out_shape=[jax.ShapeDtypeStruct((T, Dg), F32)] * 2,
        scratch_shapes=[st, st],
        compiler_params=_cp(("parallel", "parallel", "arbitrary")),
        name="delta_net",
    )(qn, kn, vn, gates, qn, kn, vn, gates)

    tr = _pick(T, (512, 256, 128))
    rows = pl.BlockSpec((tr, Dg), lambda r: (r, 0))
    return pl.pallas_call(
        functools.partial(_dn_out_kernel, H=H),
        grid=(T // tr,),
        in_specs=[rows, rows, pl.BlockSpec((tr, Dg), lambda r: (r, 10)), pl.BlockSpec((1, HEAD_DIM), lambda r: (0, 0))],
        out_specs=rows,
        out_shape=jax.ShapeDtypeStruct((T, Dg), BF16),
        compiler_params=_cp(("parallel",)),
        name="dn_out",
    )(of, ob, P, out_g.reshape(1, HEAD_DIM))


def _rope_tables(n):
    pairs = HEAD_DIM // 4
    pos = jnp.arange(n)
    row = (pos // GRID_W).astype(F32)
    col = (pos % GRID_W).astype(F32)
    inv = ROPE_THETA ** (-jnp.arange(pairs, dtype=F32) / pairs)
    ang_r, ang_c = row[:, None] * inv, col[:, None] * inv
    cos = jnp.concatenate([jnp.cos(ang_r)] * 2 + [jnp.cos(ang_c)] * 2, axis=1)
    sin = jnp.concatenate([jnp.sin(ang_r)] * 2 + [jnp.sin(ang_c)] * 2, axis=1)
    first = (jnp.arange(HEAD_DIM) % (2 * pairs)) < pairs
    return cos, jnp.where(first, -sin, 0.0), jnp.where(first, 0.0, sin)


def _permute_w_in(w, Dg, Dkv):
    o_ak = 5 * Dg
    o_lx = o_ak + 2 * Dkv
    o_dq = o_lx + 2 * Dg
    cols = [w[:, :o_ak], w[:, o_lx:o_dq + 4 * Dg], w[:, o_ak:o_lx], w[:, o_dq + 4 * Dg:]]
    w = jnp.concatenate(cols, axis=1)
    n = w.shape[1]
    tile = 1024 if n >= 4096 else 128
    pad = (-n) % tile
    return jnp.pad(w, ((0, 0), (0, pad))).astype(BF16)


def kernel(x_prompt, x_sample, c_prompt, c_sample, ada_w, ada_b, norm_g, w_in, ret_decay, ret_out_g, attn_qk_g,
           attn_out_g, lru_conv_w, lru_conv_b, lru_gate_w, lru_gate_b, lru_lambda, lru_out_g, dn_conv_w, dn_a_log,
           dn_dt_bias, dn_out_g, w_out, ffn_w_gate, ffn_w_up, ffn_conv_w, ffn_conv_b, ffn_w_down):
    b1, N, D = x_prompt.shape
    B = b1 + x_sample.shape[0]
    L = ada_w.shape[0]
    Dg = D // 4
    H = Dg // HEAD_DIM
    Dkv = max(H // 4, 1) * HEAD_DIM
    assert x_sample.shape[1] == N and N % GRID_W == 0 and Dg % HEAD_DIM == 0 and B <= 8

    x = jnp.concatenate([x_prompt.reshape(b1 * N, D), x_sample.reshape(-1, D)], axis=0)
    c8 = jnp.zeros((8, D), F32).at[:B].set(jnp.concatenate([c_prompt, c_sample], axis=0))
    mods = _ada(c8, ada_w, ada_b)
    tabs = _rope_tables(N)

    f = None
    for l in range(L):
        mod = mods[l]
        if l == 0:
            h, = _resnorm(x, B, N, gb=norm_g[l, 0], scale=(mod, 1), shift=(mod, 0))
        else:
            x, h = _resnorm(x, B, N, f=f, ga=norm_g[l - 1, 3], gate=(mods[l - 1], 5),
                            gb=norm_g[l, 0], scale=(mod, 1), shift=(mod, 0))
        P = _mm(h, _permute_w_in(w_in[l], Dg, Dkv), F32, "mm_in")
        ret = _retention(P, tabs, ret_decay[l], ret_out_g[l], B, N, Dg)
        att = _attention(P, tabs, attn_qk_g[l], attn_out_g[l], B, N, Dg)
        rec = _rg_lru(P, lru_conv_w[l], lru_conv_b[l], lru_gate_w[l], lru_gate_b[l], lru_lambda[l], lru_out_g[l], B, N, Dg)
        dn = _delta_net(P, dn_conv_w[l], dn_a_log[l], dn_dt_bias[l], dn_out_g[l], B, N, Dg)
        mixed = _mm_mixed([ret, att, rec, dn], w_out[l].astype(BF16))
        x, h = _resnorm(x, B, N, f=mixed, ga=norm_g[l, 1], gate=(mod, 2), gb=norm_g[l, 2], scale=(mod, 4), shift=(mod, 3))
        a = _ffn_gu(h, ffn_w_gate[l].astype(BF16), ffn_w_up[l].astype(BF16), ffn_conv_w[l], ffn_conv_b[l], N)
        f = _mm(a, ffn_w_down[l].astype(BF16), F32, "mm_down")
    x, = _resnorm(x, B, N, f=f, ga=norm_g[L - 1, 3], gate=(mods[L - 1], 5))
    return x[:b1 * N].reshape(b1, N, D), x[b1 * N:].reshape(B - b1, N, D)
```

```python
import functools
import math

import jax
import jax.numpy as jnp
from jax import lax
from jax.experimental import pallas as pl
from jax.experimental.pallas import tpu as pltpu

F32 = jnp.float32
BF16 = jnp.bfloat16

HEAD_DIM = 128
GRID_W = 64
ROPE_THETA = 10000.0
EPS = 1e-6
LRU_C = 8.0
DN_CHUNK = 64
V7X_VMEM_LIMIT = 56 * 1024 * 1024
NEG_BIG = -1e30
Q_SCALE_LOG2 = HEAD_DIM ** -0.5 * math.log2(math.e)
FLASH_ROWS = 32


def _cp(sem, vmem=V7X_VMEM_LIMIT):
    return pltpu.CompilerParams(dimension_semantics=sem, vmem_limit_bytes=vmem)


def _pick(n, cands):
    for c in cands:
        if n % c == 0:
            return c
    return n


def _bdot(a, b):
    return jnp.dot(a.astype(BF16), b.astype(BF16), preferred_element_type=F32)


def _bdot_nt(a, b):
    return lax.dot_general(a.astype(BF16), b.astype(BF16), (((1,), (1,)), ((), ())), preferred_element_type=F32)


def _bdot_tn(a, b):
    return lax.dot_general(a.astype(BF16), b.astype(BF16), (((0,), (0,)), ((), ())), preferred_element_type=F32)


def _rms(x, g):
    return x * lax.rsqrt(jnp.mean(x * x, axis=-1, keepdims=True) + EPS) * g


def _rope(x, c, s1, s2):
    return x * c + pltpu.roll(x, 96, 1) * s1 + pltpu.roll(x, 32, 1) * s2


def _neg_expm1(z):
    series = -z * (1.0 + z * 0.5 * (1.0 + z * (1.0 / 3.0) * (1.0 + z * 0.25 * (1.0 + z * 0.2))))
    return jnp.where(z > -0.02, series, 1.0 - jnp.exp(z))


def _ada_kernel(c_ref, w_ref, b_ref, o_ref):
    c = c_ref[...]
    o_ref[...] = _bdot(c * jax.nn.sigmoid(c), w_ref[...]) + b_ref[...]


def _ada(c8, ada_w, ada_b):
    L, D, M = ada_w.shape
    tn = _pick(M, (512, 256, 128))
    return pl.pallas_call(
        _ada_kernel,
        grid=(L, M // tn),
        in_specs=[pl.BlockSpec((8, D), lambda l, j: (0, 0)),
                  pl.BlockSpec((None, D, tn), lambda l, j: (l, 0, j)),
                  pl.BlockSpec((None, 1, tn), lambda l, j: (l, 0, j))],
        out_specs=pl.BlockSpec((None, 8, tn), lambda l, j: (l, 0, j)),
        out_shape=jax.ShapeDtypeStruct((L, 8, M), F32),
        compiler_params=_cp(("parallel", "parallel")),
        name="ada_mod",
    )(c8, ada_w, ada_b.reshape(L, 1, M))


def _resnorm_kernel(*refs, has_f, has_h):
    it = iter(refs)
    x_ref = next(it)
    if has_f:
        f_ref, ga_ref, gate_ref = next(it), next(it), next(it)
    if has_h:
        gb_ref, scale_ref, shift_ref = next(it), next(it), next(it)
    if has_f:
        xo_ref = next(it)
    if has_h:
        h_ref = next(it)
    x = x_ref[...]
    if has_f:
        x = x + gate_ref[...] * _rms(f_ref[...], ga_ref[...])
        xo_ref[...] = x
    if has_h:
        h_ref[...] = (_rms(x, gb_ref[...]) * (1.0 + scale_ref[...]) + shift_ref[...]).astype(h_ref.dtype)


def _resnorm(x, B, N, *, f=None, ga=None, gate=None, gb=None, scale=None, shift=None):
    T, D = x.shape
    tr = _pick(N, (256, 128, 64, 8))
    nt = N // tr
    has_f, has_h = f is not None, gb is not None
    row = pl.BlockSpec((tr, D), lambda b, t: (b * nt + t, 0))
    gain = pl.BlockSpec((1, D), lambda b, t: (0, 0))

    def modarg(mp):
        mod, part = mp
        return mod.reshape(mod.shape[0], 1, mod.shape[1]), pl.BlockSpec((None, 1, D), lambda b, t: (b, 0, part))

    args, specs = [x], [row]
    if has_f:
        m, sp = modarg(gate)
        args += [f, ga.reshape(1, D), m]
        specs += [row, gain, sp]
    if has_h:
        (m1, sp1), (m2, sp2) = modarg(scale), modarg(shift)
        args += [gb.reshape(1, D), m1, m2]
        specs += [gain, sp1, sp2]
    out_shape, out_specs = [], []
    if has_f:
        out_shape.append(jax.ShapeDtypeStruct((T, D), F32))
        out_specs.append(row)
    if has_h:
        out_shape.append(jax.ShapeDtypeStruct((T, D), BF16))
        out_specs.append(row)
    return pl.pallas_call(
        functools.partial(_resnorm_kernel, has_f=has_f, has_h=has_h),
        grid=(B, nt), in_specs=specs, out_specs=out_specs, out_shape=out_shape,
        compiler_params=_cp(("parallel", "parallel")),
        name="resnorm",
    )(*args)


def _mm_kernel(a_ref, w_ref, o_ref, *acc, nk):
    if nk == 1:
        o_ref[...] = jnp.dot(a_ref[...], w_ref[...], preferred_element_type=F32).astype(o_ref.dtype)
        return
    acc_ref, = acc
    k = pl.program_id(2)
    part = jnp.dot(a_ref[...], w_ref[...], preferred_element_type=F32)

    @pl.when(k == 0)
    def _():
        acc_ref[...] = part

    @pl.when(jnp.logical_and(k > 0, k < nk - 1))
    def _():
        acc_ref[...] += part

    @pl.when(k == nk - 1)
    def _():
        o_ref[...] = (acc_ref[...] + part).astype(o_ref.dtype)


def _split_k(K):
    nk = 1
    while K // nk > 6144 or K % nk or (K // nk) % 128:
        nk += 1
    return nk


def _mm(a, w, out_dtype, name):
    M, K = a.shape
    N = w.shape[1]
    nk = _split_k(K)
    tk = K // nk
    tm = _pick(M, (1024, 512, 256, 128))
    tn = _pick(N, (1024, 512, 256, 128)) if nk == 1 else _pick(N, (512, 256, 128))
    return pl.pallas_call(
        functools.partial(_mm_kernel, nk=nk),
        grid=(M // tm, N // tn, nk),
        in_specs=[pl.BlockSpec((tm, tk), lambda i, j, k: (i, k)),
                  pl.BlockSpec((tk, tn), lambda i, j, k: (k, j))],
        out_specs=pl.BlockSpec((tm, tn), lambda i, j, k: (i, j)),
        out_shape=jax.ShapeDtypeStruct((M, N), out_dtype),
        scratch_shapes=[pltpu.VMEM((tm, tn), F32)] if nk > 1 else [],
        compiler_params=_cp(("parallel", "parallel", "arbitrary")),
        name=name,
    )(a, w)


def _mm4_kernel(a0, a1, a2, a3, w_ref, o_ref, *, dg):
    acc = jnp.dot(a0[...], w_ref[0:dg, :], preferred_element_type=F32)
    acc += jnp.dot(a1[...], w_ref[dg:2 * dg, :], preferred_element_type=F32)
    acc += jnp.dot(a2[...], w_ref[2 * dg:3 * dg, :], preferred_element_type=F32)
    acc += jnp.dot(a3[...], w_ref[3 * dg:4 * dg, :], preferred_element_type=F32)
    o_ref[...] = acc


def _mm_mixed(parts, w):
    M, dg = parts[0].shape
    N = w.shape[1]
    tm = _pick(M, (1024, 512, 256, 128))
    tn = _pick(N, (1024, 512, 256, 128))
    a_spec = pl.BlockSpec((tm, dg), lambda i, j: (i, 0))
    return pl.pallas_call(
        functools.partial(_mm4_kernel, dg=dg),
        grid=(M // tm, N // tn),
        in_specs=[a_spec] * 4 + [pl.BlockSpec((4 * dg, tn), lambda i, j: (0, j))],
        out_specs=pl.BlockSpec((tm, tn), lambda i, j: (i, j)),
        out_shape=jax.ShapeDtypeStruct((M, N), F32),
        compiler_params=_cp(("parallel", "parallel")),
        name="mm_out",
    )(*parts, w)


FFN_HALO = 16


def _ffn_gu_kernel(h_ref, hp_ref, hn_ref, wg_ref, wu_ref, cw_ref, cb_ref, o_ref, hext_ref, g_ref, *, tm, tiles_per_seq):
    i, j = pl.program_id(0), pl.program_id(1)

    @pl.when(j == 0)
    def _():
        hext_ref[0:FFN_HALO, :] = hp_ref[...]
        hext_ref[FFN_HALO:FFN_HALO + tm, :] = h_ref[...]
        hext_ref[FFN_HALO + tm:, :] = hn_ref[...]

    g_ref[...] = jnp.dot(hext_ref[...], wg_ref[...], preferred_element_type=F32)
    up = jnp.dot(h_ref[...], wu_ref[...], preferred_element_type=F32)
    row = lax.broadcasted_iota(jnp.int32, (tm, 1), 0)
    first = (i % tiles_per_seq) == 0
    last = (i % tiles_per_seq) == tiles_per_seq - 1
    g_prev = jnp.where(jnp.logical_and(first, row == 0), 0.0, g_ref[FFN_HALO - 1:FFN_HALO - 1 + tm, :])
    g_next = jnp.where(jnp.logical_and(last, row == tm - 1), 0.0, g_ref[FFN_HALO + 1:FFN_HALO + 1 + tm, :])
    g_cur = g_ref[FFN_HALO:FFN_HALO + tm, :]
    cw = cw_ref[...]
    gate = cw[0:1, :] * g_prev + cw[1:2, :] * g_cur + cw[2:3, :] * g_next + cb_ref[...]
    o_ref[...] = (jax.nn.gelu(gate, approximate=True) * up).astype(o_ref.dtype)


def _ffn_gu(h, wg, wu, cw, cb, N):
    T, D = h.shape
    F = wg.shape[1]
    tm = _pick(N, (512, 256, 128))
    tn = _pick(F, (512, 256, 128))
    tps = N // tm
    hb = tm // FFN_HALO
    nhb = T // FFN_HALO
    return pl.pallas_call(
        functools.partial(_ffn_gu_kernel, tm=tm, tiles_per_seq=tps),
        grid=(T // tm, F // tn),
        in_specs=[pl.BlockSpec((tm, D), lambda i, j: (i, 0)),
                  pl.BlockSpec((FFN_HALO, D), lambda i, j: (jnp.maximum(i * hb - 1, 0), 0)),
                  pl.BlockSpec((FFN_HALO, D), lambda i, j: (jnp.minimum((i + 1) * hb, nhb - 1), 0)),
                  pl.BlockSpec((D, tn), lambda i, j: (0, j)),
                  pl.BlockSpec((D, tn), lambda i, j: (0, j)),
                  pl.BlockSpec((3, tn), lambda i, j: (0, j)),
                  pl.BlockSpec((1, tn), lambda i, j: (0, j))],
        out_specs=pl.BlockSpec((tm, tn), lambda i, j: (i, j)),
        out_shape=jax.ShapeDtypeStruct((T, F), BF16),
        scratch_shapes=[pltpu.VMEM((tm + 2 * FFN_HALO, D), BF16), pltpu.VMEM((tm + 2 * FFN_HALO, tn), F32)],
        compiler_params=_cp(("parallel", "arbitrary")),
        name="ffn_gate_up",
    )(h, h, h, wg, wu, cw, cb.reshape(1, F))


def _ret_decays(dec_ref, d, h):
    x = dec_ref[d, h]
    return jax.nn.log_sigmoid(x)


def _ret_bwd_kernel(q_ref, k_ref, v_ref, c_ref, s1_ref, s2_ref, dec_ref, ob_ref, st_ref, *, H, C):
    @pl.when(pl.program_id(1) == 0)
    def _():
        st_ref[...] = jnp.zeros_like(st_ref)

    cs, s1, s2 = c_ref[...], s1_ref[...], s2_ref[...]
    idx = lax.broadcasted_iota(jnp.int32, (C, 1), 0).astype(F32)
    for h in range(H):
        sl = slice(h * HEAD_DIM, (h + 1) * HEAD_DIM)
        lg = _ret_decays(dec_ref, 1, h)
        q = _rope(q_ref[:, sl], cs, s1, s2)
        k = _rope(k_ref[:, sl], cs, s1, s2) * (HEAD_DIM ** -0.5)
        state = st_ref[h]
        ob_ref[:, sl] = _bdot(q * jnp.exp(lg * (C - idx)), state)
        st_ref[h] = jnp.exp(lg * C) * state + _bdot_tn(k * jnp.exp(lg * idx), v_ref[:, sl])


def _ret_fwd_kernel(q_ref, k_ref, v_ref, g_ref, ob_ref, c_ref, s1_ref, s2_ref, dec_ref, og_ref, o_ref, st_ref, *, H, C):
    @pl.when(pl.program_id(1) == 0)
    def _():
        st_ref[...] = jnp.zeros_like(st_ref)

    cs, s1, s2 = c_ref[...], s1_ref[...], s2_ref[...]
    idx = lax.broadcasted_iota(jnp.int32, (C, 1), 0).astype(F32)
    diff = (lax.broadcasted_iota(jnp.int32, (C, C), 0) - lax.broadcasted_iota(jnp.int32, (C, C), 1)).astype(F32)
    for h in range(H):
        sl = slice(h * HEAD_DIM, (h + 1) * HEAD_DIM)
        lgf = _ret_decays(dec_ref, 0, h)
        lgb = _ret_decays(dec_ref, 1, h)
        q = _rope(q_ref[:, sl], cs, s1, s2)
        k = _rope(k_ref[:, sl], cs, s1, s2) * (HEAD_DIM ** -0.5)
        v = v_ref[:, sl]
        decay = jnp.exp(jnp.where(diff >= 0, lgf * diff, -lgb * diff))
        scores = _bdot_nt(q, k) * decay
        state = st_ref[h]
        o = _bdot(scores, v) + _bdot(q * jnp.exp(lgf * (idx + 1.0)), state) + ob_ref[:, sl]
        st_ref[h] = jnp.exp(lgf * C) * state + _bdot_tn(k * jnp.exp(lgf * (C - 1.0 - idx)), v)
        g = g_ref[:, sl]
        o_ref[:, sl] = (_rms(o, og_ref[:, sl]) * (g * jax.nn.sigmoid(g))).astype(o_ref.dtype)


def _retention(P, tabs, ret_decay, ret_out_g, B, N, Dg):
    T = P.shape[0]
    H = Dg // HEAD_DIM
    C = _pick(N, (256, 128, 64))
    nc = N // C
    dec = ret_decay.reshape(2, H, 1, 1)
    dec_spec = pl.BlockSpec((2, H, 1, 1), lambda b, c: (0, 0, 0, 0))
    st = pltpu.VMEM((H, HEAD_DIM, HEAD_DIM), F32)

    def seg(s, rev):
        if rev:
            return pl.BlockSpec((C, Dg), lambda b, c: (b * nc + nc - 1 - c, s))
        return pl.BlockSpec((C, Dg), lambda b, c: (b * nc + c, s))

    def tab(rev):
        if rev:
            return pl.BlockSpec((C, HEAD_DIM), lambda b, c: (nc - 1 - c, 0))
        return pl.BlockSpec((C, HEAD_DIM), lambda b, c: (c, 0))

    ob = pl.pallas_call(
        functools.partial(_ret_bwd_kernel, H=H, C=C),
        grid=(B, nc),
        in_specs=[seg(0, True), seg(1, True), seg(2, True), tab(True), tab(True), tab(True), dec_spec],
        out_specs=seg(0, True),
        out_shape=jax.ShapeDtypeStruct((T, Dg), F32),
        scratch_shapes=[st],
        compiler_params=_cp(("parallel", "arbitrary")),
        name="ret_bwd",
    )(P, P, P, *tabs, dec)
    return pl.pallas_call(
        functools.partial(_ret_fwd_kernel, H=H, C=C),
        grid=(B, nc),
        in_specs=[seg(0, False), seg(1, False), seg(2, False), seg(3, False), seg(0, False),
                  tab(False), tab(False), tab(False), dec_spec, pl.BlockSpec((1, Dg), lambda b, c: (0, 0))],
        out_specs=seg(0, False),
        out_shape=jax.ShapeDtypeStruct((T, Dg), BF16),
        scratch_shapes=[st],
        compiler_params=_cp(("parallel", "arbitrary")),
        name="ret_fwd",
    )(P, P, P, P, ob, *tabs, dec, ret_out_g.reshape(1, Dg))


def _att_prep_kernel(q_ref, k_ref, v_ref, c_ref, s1_ref, s2_ref, g_ref, qo_ref, ko_ref, vo_ref, *, H, Hkv):
    cs, s1, s2 = c_ref[...], s1_ref[...], s2_ref[...]
    gq, gk = g_ref[0:1, :], g_ref[1:2, :]
    for h in range(H):
        sl = slice(h * HEAD_DIM, (h + 1) * HEAD_DIM)
        qo_ref[:, sl] = (_rope(_rms(q_ref[:, sl], gq), cs, s1, s2) * Q_SCALE_LOG2).astype(qo_ref.dtype)
    for h in range(Hkv):
        sl = slice(h * HEAD_DIM, (h + 1) * HEAD_DIM)
        ko_ref[:, sl] = _rope(_rms(k_ref[:, sl], gk), cs, s1, s2).astype(ko_ref.dtype)
    vo_ref[...] = v_ref[...].astype(vo_ref.dtype)


def _flash_kernel(q_ref, k_ref, v_ref, og_ref, o_ref, m_ref, l_ref, acc_ref, a_ref, s_ref, p_ref, *, grp, nk, rc):
    ki = pl.program_id(3)

    @pl.when(ki == 0)
    def _():
        m_ref[...] = jnp.full_like(m_ref, -jnp.inf)
        l_ref[...] = jnp.zeros_like(l_ref)
        acc_ref[...] = jnp.zeros_like(acc_ref)

    tq, tk = q_ref.shape[0], k_ref.shape[0]
    rep = tk // HEAD_DIM
    for g in range(grp):
        sl = slice(g * HEAD_DIM, (g + 1) * HEAD_DIM)
        s_ref[g] = lax.dot_general(q_ref[:, sl], k_ref[...], (((1,), (1,)), ((), ())), preferred_element_type=F32)
        for r in range(tq // rc):
            rows = slice(r * rc, (r + 1) * rc)
            s = s_ref[g, rows, :]
            m_old = m_ref[g, rows, :]
            m_new = jnp.maximum(m_old, jnp.max(s, axis=-1, keepdims=True))
            alpha = jnp.exp2(m_old - m_new)
            p = jnp.exp2(s - pltpu.repeat(m_new, rep, 1))
            l_ref[g, rows, :] = alpha * l_ref[g, rows, :] + jnp.sum(p, axis=-1, keepdims=True)
            p_ref[g, rows, :] = p.astype(BF16)
            a_ref[g, rows, :] = alpha
            m_ref[g, rows, :] = m_new
        acc_ref[g] = a_ref[g] * acc_ref[g] + jnp.dot(p_ref[g], v_ref[...], preferred_element_type=F32)

    @pl.when(ki == nk - 1)
    def _():
        for g in range(grp):
            sl = slice(g * HEAD_DIM, (g + 1) * HEAD_DIM)
            o = acc_ref[g] / l_ref[g]
            o_ref[:, sl] = _rms(o, og_ref[:, sl]).astype(o_ref.dtype)


def _attention(P, tabs, attn_qk_g, attn_out_g, B, N, Dg):
    T = P.shape[0]
    H = Dg // HEAD_DIM
    Hkv = max(H // 4, 1)
    grp = H // Hkv
    Dkv = Hkv * HEAD_DIM
    tr = _pick(N, (512, 256, 128))
    nrt = N // tr
    tab = pl.BlockSpec((tr, HEAD_DIM), lambda r: (r % nrt, 0))
    qa, ka, va = pl.pallas_call(
        functools.partial(_att_prep_kernel, H=H, Hkv=Hkv),
        grid=(T // tr,),
        in_specs=[pl.BlockSpec((tr, Dg), lambda r: (r, 4)),
                  pl.BlockSpec((tr, Dkv), lambda r: (r, 11 * grp)),
                  pl.BlockSpec((tr, Dkv), lambda r: (r, 11 * grp + 1)),
                  tab, tab, tab,
                  pl.BlockSpec((2, HEAD_DIM), lambda r: (0, 0))],
        out_specs=[pl.BlockSpec((tr, Dg), lambda r: (r, 0)),
                   pl.BlockSpec((tr, Dkv), lambda r: (r, 0)),
                   pl.BlockSpec((tr, Dkv), lambda r: (r, 0))],
        out_shape=[jax.ShapeDtypeStruct((T, Dg), BF16), jax.ShapeDtypeStruct((T, Dkv), BF16),
                   jax.ShapeDtypeStruct((T, Dkv), BF16)],
        compiler_params=_cp(("parallel",)),
        name="att_prep",
    )(P, P, P, *tabs, attn_qk_g)

    return _flash(qa, ka, va, attn_out_g, B, N, Dg)


def _flash(qa, ka, va, attn_out_g, B, N, Dg):
    T = qa.shape[0]
    H = Dg // HEAD_DIM
    Hkv = max(H // 4, 1)
    grp = H // Hkv
    tq = _pick(N, (512, 256, 128))
    tk = _pick(N, (2048, 1024, 512, 256, 128))
    nq, nk = N // tq, N // tk
    gw = grp * HEAD_DIM
    return pl.pallas_call(
        functools.partial(_flash_kernel, grp=grp, nk=nk, rc=min(FLASH_ROWS, tq)),
        grid=(B, Hkv, nq, nk),
        in_specs=[pl.BlockSpec((tq, gw), lambda b, h, qi, ki: (b * nq + qi, h)),
                  pl.BlockSpec((tk, HEAD_DIM), lambda b, h, qi, ki: (b * nk + ki, h)),
                  pl.BlockSpec((tk, HEAD_DIM), lambda b, h, qi, ki: (b * nk + ki, h)),
                  pl.BlockSpec((1, gw), lambda b, h, qi, ki: (0, h))],
        out_specs=pl.BlockSpec((tq, gw), lambda b, h, qi, ki: (b * nq + qi, h)),
        out_shape=jax.ShapeDtypeStruct((T, Dg), BF16),
        scratch_shapes=[pltpu.VMEM((grp, tq, HEAD_DIM), F32)] * 4
        + [pltpu.VMEM((grp, tq, tk), F32), pltpu.VMEM((grp, tq, tk), BF16)],
        compiler_params=_cp(("parallel", "parallel", "parallel", "arbitrary")),
        name="flash_attn",
    )(qa, ka, va, attn_out_g.reshape(1, Dg))


def _conv4(cur, prev8, next8, w, n):
    win = jnp.concatenate([prev8, cur, next8], axis=0)
    return (w[0:1, :] * win[6:6 + n] + w[1:2, :] * win[7:7 + n]
            + w[2:3, :] * win[8:8 + n] + w[3:4, :] * win[9:9 + n])


def _scan_steps(a, u, n, reverse):
    row = lax.broadcasted_iota(jnp.int32, (n, 1), 0)
    d = 1
    while d < n:
        if d < 8:
            if reverse:
                keep = row < n - d
                a_s, u_s = pltpu.roll(a, n - d, 0), pltpu.roll(u, n - d, 0)
            else:
                keep = row >= d
                a_s, u_s = pltpu.roll(a, d, 0), pltpu.roll(u, d, 0)
            a_s = jnp.where(keep, a_s, 1.0)
            u_s = jnp.where(keep, u_s, 0.0)
        else:
            ones = jnp.ones((d, a.shape[1]), F32)
            zeros = jnp.zeros((d, a.shape[1]), F32)
            if reverse:
                a_s = jnp.concatenate([a[d:], ones], axis=0)
                u_s = jnp.concatenate([u[d:], zeros], axis=0)
            else:
                a_s = jnp.concatenate([ones, a[:n - d]], axis=0)
                u_s = jnp.concatenate([zeros, u[:n - d]], axis=0)
        u = a * u_s + u
        a = a * a_s
        d *= 2
    return a, u


def _lru_kernel(x_ref, gt_ref, cw_ref, cb_ref, gw_ref, gb_ref, lam_ref, og_ref, o_ref, hf_ref, *, N, Tc):
    nc = N // Tc
    cw = cw_ref[...]
    cb = cb_ref[...]
    W = x_ref.shape[1]

    def conv_chunk(c):
        t0 = pl.multiple_of(c * Tc, Tc)
        cur = x_ref[pl.ds(t0, Tc), :]
        p0 = pl.multiple_of(jnp.maximum(t0 - 8, 0), 8)
        n0 = pl.multiple_of(jnp.minimum(t0 + Tc, N - 8), 8)
        prev8 = jnp.where(c > 0, x_ref[pl.ds(p0, 8), :], 0.0)
        next8 = jnp.where(c < nc - 1, x_ref[pl.ds(n0, 8), :], 0.0)
        return _conv4(cur, prev8, next8, cw, Tc) + cb

    def gates(x, d):
        z = jnp.dot(x.astype(BF16), gw_ref[d], preferred_element_type=F32) + gb_ref[d]
        r = jax.nn.sigmoid(z[:, :W])
        i = jax.nn.sigmoid(z[:, W:])
        lam = lam_ref[d:d + 1, :]
        log_a = -LRU_C * r * jax.nn.softplus(-lam)
        a = jnp.exp(log_a)
        u = jnp.sqrt(_neg_expm1(2.0 * log_a)) * (i * x)
        return a, u

    def fwd(c, carry):
        x = conv_chunk(c)
        a, u = gates(x, 0)
        a, u = _scan_steps(a, u, Tc, False)
        h = u + a * carry
        hf_ref[pl.ds(pl.multiple_of(c * Tc, Tc), Tc), :] = h
        return h[Tc - 1:Tc, :]

    lax.fori_loop(0, nc, fwd, jnp.zeros((1, W), F32))

    def bwd(ci, carry):
        c = nc - 1 - ci
        t0 = pl.multiple_of(c * Tc, Tc)
        x = conv_chunk(c)
        a, u = gates(x, 1)
        a, u = _scan_steps(a, u, Tc, True)
        h = u + a * carry
        rec = (hf_ref[pl.ds(t0, Tc), :] + h) * jax.nn.gelu(gt_ref[pl.ds(t0, Tc), :], approximate=True)
        o_ref[pl.ds(t0, Tc), :] = _rms(rec, og_ref[...]).astype(o_ref.dtype)
        return h[0:1, :]

    lax.fori_loop(0, nc, bwd, jnp.zeros((1, W), F32))


def _rg_lru(P, conv_w, conv_b, gate_w, gate_b, lam, out_g, B, N, Dg):
    T = P.shape[0]
    W = HEAD_DIM
    K = Dg // W
    Tc = _pick(N, (256, 128, 64))
    gw = jnp.concatenate([gate_w[:, 0], gate_w[:, 1]], axis=-1).astype(BF16)
    gb = jnp.concatenate([gate_b[:, 0], gate_b[:, 1]], axis=-1).reshape(2, K, 1, 2 * W)
    base = Dg // W
    return pl.pallas_call(
        functools.partial(_lru_kernel, N=N, Tc=Tc),
        grid=(B, K),
        in_specs=[pl.BlockSpec((N, W), lambda b, k: (b, 5 * base + k)),
                  pl.BlockSpec((N, W), lambda b, k: (b, 6 * base + k)),
                  pl.BlockSpec((4, W), lambda b, k: (0, k)),
                  pl.BlockSpec((1, W), lambda b, k: (0, k)),
                  pl.BlockSpec((2, None, W, 2 * W), lambda b, k: (0, k, 0, 0)),
                  pl.BlockSpec((2, None, 1, 2 * W), lambda b, k: (0, k, 0, 0)),
                  pl.BlockSpec((2, W), lambda b, k: (0, k)),
                  pl.BlockSpec((1, W), lambda b, k: (0, k))],
        out_specs=pl.BlockSpec((N, W), lambda b, k: (b, k)),
        out_shape=jax.ShapeDtypeStruct((T, Dg), BF16),
        scratch_shapes=[pltpu.VMEM((N, W), F32)],
        compiler_params=_cp(("parallel", "parallel")),
        name="rg_lru",
    )(P, P, conv_w, conv_b.reshape(1, Dg), gw, gb, lam, out_g.reshape(1, Dg))


def _dn_prep_kernel(x_ref, xp_ref, xn_ref, w_ref, o_ref, *, n, nt, H, mode):
    t = pl.program_id(1)
    prev8 = jnp.where(t > 0, xp_ref[...], 0.0)
    next8 = jnp.where(t < nt - 1, xn_ref[...], 0.0)
    y = _conv4(x_ref[...], prev8, next8, w_ref[...], n)
    y = y * jax.nn.sigmoid(y)
    if mode == "v":
        o_ref[...] = y
        return
    scale = HEAD_DIM ** -0.5 if mode == "q" else 1.0
    for h in range(H):
        sl = slice(h * HEAD_DIM, (h + 1) * HEAD_DIM)
        yh = y[:, sl]
        o_ref[:, sl] = yh * (lax.rsqrt(jnp.sum(yh * yh, axis=-1, keepdims=True) + EPS) * scale)


def _dn_gates_kernel(g_ref, al_ref, dt_ref, o_ref, *, H):
    raw = g_ref[...]
    n = raw.shape[0]
    lane = lax.broadcasted_iota(jnp.int32, (n, HEAD_DIM), 1)
    for h in range(H):
        cols = []
        for d in range(2):
            beta = jax.nn.sigmoid(raw[:, d * H + h:d * H + h + 1])
            a_in = raw[:, (2 + d) * H + h:(2 + d) * H + h + 1]
            g = -jnp.exp(jnp.full((1, 1), al_ref[d, h], F32)) * jax.nn.softplus(a_in + dt_ref[d, h])
            cols += [beta, g]
        bf, gf, bb, gb = cols
        slab = jnp.where(lane == 0, bf, jnp.where(lane == 1, bb, jnp.where(lane == 2, gf, jnp.where(lane == 3, gb, 0.0))))
        o_ref[:, h * HEAD_DIM:(h + 1) * HEAD_DIM] = slab


def _bmm(a, b):
    return jnp.einsum('cij,cjk->cik', a.astype(BF16), b.astype(BF16), preferred_element_type=F32)


def _bmm_nt(a, b):
    return jnp.einsum('cik,cjk->cij', a.astype(BF16), b.astype(BF16), preferred_element_type=F32)


def _bmm_tn(a, b):
    return jnp.einsum('cjd,cjv->cdv', a.astype(BF16), b.astype(BF16), preferred_element_type=F32)


def _dn_block_prep(q, k, v, beta, g, reverse, nb, C):
    R, Dh = q.shape
    ri = lax.broadcasted_iota(jnp.int32, (C, C), 0)
    ci = lax.broadcasted_iota(jnp.int32, (C, C), 1)
    incl = ((ri <= ci) if reverse else (ri >= ci))[None]
    strict = ((ri < ci) if reverse else (ri > ci))[None]
    vis = ((ri >= ci) if reverse else (ri <= ci))[None]
    eye = (ri == ci).astype(F32)[None]
    gb = jnp.broadcast_to(g, (R, Dh))
    pos = lax.broadcasted_iota(jnp.int32, (R, 1), 0) & (C - 1)
    gcol = gb
    d = 1
    while d < C:
        if reverse:
            gcol = gcol + jnp.where(pos < C - d, pltpu.roll(gcol, R - d, 0), 0.0)
        else:
            gcol = gcol + jnp.where(pos >= d, pltpu.roll(gcol, d, 0), 0.0)
        d *= 2
    g3 = gcol.reshape(nb, C, Dh)
    grow = jnp.sum(jnp.where(vis, gb.reshape(nb, C, Dh)[:, :, :C], 0.0), axis=1, keepdims=True)
    decay = jnp.exp(jnp.where(incl, g3[:, :, :C] - grow, NEG_BIG))
    kb = k * beta
    k3, q3 = k.reshape(nb, C, Dh), q.reshape(nb, C, Dh)
    m = jnp.where(strict, _bmm_nt(kb.reshape(nb, C, Dh), k3) * decay, 0.0)
    p = -m
    tinv = eye + p
    for _ in range(int(math.log2(C)) - 1):
        p = _bmm(p, p)
        tinv = tinv + _bmm(tinv, p)
    eg = jnp.exp(gcol)
    rhs = jnp.concatenate([v * beta, kb * eg], axis=1).reshape(nb, C, 2 * Dh)
    uw = _bmm(tinv, rhs)
    aqk = _bmm_nt(q3, k3) * decay
    glast = g3[:, 0:1, :] if reverse else g3[:, C - 1:C, :]
    ktail = k3 * jnp.exp(glast - g3)
    cdecay = jnp.exp(glast)
    kt_uw = _bmm_tn(ktail, uw)
    a_uw = _bmm(aqk, uw)
    o0 = a_uw[:, :, :Dh]
    qp = (q * eg).reshape(nb, C, Dh) - a_uw[:, :, Dh:]
    return o0, qp.astype(BF16), kt_uw[:, :, :Dh], kt_uw[:, :, Dh:].astype(BF16), cdecay


def _dn_kernel(qf_ref, kf_ref, vf_ref, gf_ref, qb_ref, kb_ref, vb_ref, gb_ref, of_ref, ob_ref, sf_ref, sb_ref, *, Tb, C):
    @pl.when(pl.program_id(2) == 0)
    def _():
        sf_ref[...] = jnp.zeros_like(sf_ref)
        sb_ref[...] = jnp.zeros_like(sb_ref)

    nb = Tb // C
    gf, gb = gf_ref[...], gb_ref[...]
    fwd = _dn_block_prep(qf_ref[...], kf_ref[...], vf_ref[...], gf[:, 0:1], gf[:, 2:3], False, nb, C)
    bwd = _dn_block_prep(qb_ref[...], kb_ref[...], vb_ref[...], gb[:, 1:2], gb[:, 3:4], True, nb, C)

    def step(prep, c, state, o_ref):
        o0, qp, bmat, mmat, cdecay = prep
        sb = state.astype(BF16)
        o_ref[c * C:(c + 1) * C, :] = o0[c] + jnp.dot(qp[c], sb, preferred_element_type=F32)
        return state * cdecay[c] + bmat[c] - jnp.dot(mmat[c], sb, preferred_element_type=F32)

    sf, sb = sf_ref[...], sb_ref[...]
    for c in range(nb):
        sf = step(fwd, c, sf, of_ref)
        sb = step(bwd, nb - 1 - c, sb, ob_ref)
    sf_ref[...] = sf
    sb_ref[...] = sb


def _dn_out_kernel(of_ref, ob_ref, z_ref, g_ref, o_ref, *, H):
    g = g_ref[...]
    for h in range(H):
        sl = slice(h * HEAD_DIM, (h + 1) * HEAD_DIM)
        z = z_ref[:, sl]
        o_ref[:, sl] = (_rms(of_ref[:, sl] + ob_ref[:, sl], g) * (z * jax.nn.sigmoid(z))).astype(o_ref.dtype)


def _delta_net(P, conv_w, a_log, dt_bias, out_g, B, N, Dg):
    T = P.shape[0]
    H = Dg // HEAD_DIM
    tp = _pick(N, (256, 128, 64))
    nt = N // tp
    hb = tp // 8
    nhb = T // 8

    def prep(seg, mode):
        return pl.pallas_call(
            functools.partial(_dn_prep_kernel, n=tp, nt=nt, H=H, mode=mode),
            grid=(B, nt),
            in_specs=[pl.BlockSpec((tp, Dg), lambda b, t: (b * nt + t, 7 + seg)),
                      pl.BlockSpec((8, Dg), lambda b, t: (jnp.maximum((b * nt + t) * hb - 1, 0), 7 + seg)),
                      pl.BlockSpec((8, Dg), lambda b, t: (jnp.minimum((b * nt + t + 1) * hb, nhb - 1), 7 + seg)),
                      pl.BlockSpec((4, Dg), lambda b, t: (0, seg))],
            out_specs=pl.BlockSpec((tp, Dg), lambda b, t: (b * nt + t, 0)),
            out_shape=jax.ShapeDtypeStruct((T, Dg), F32),
            compiler_params=_cp(("parallel", "parallel")),
            name="dn_prep_" + mode,
        )(P, P, P, conv_w)

    qn, kn, vn = prep(0, "q"), prep(1, "k"), prep(2, "v")

    tg = _pick(T, (512, 256, 128))
    gcol = (11 * Dg + 2 * max(H // 4, 1) * HEAD_DIM) // HEAD_DIM
    smem = pl.BlockSpec(memory_space=pltpu.SMEM)
    gates = pl.pallas_call(
        functools.partial(_dn_gates_kernel, H=H),
        grid=(T // tg,),
        in_specs=[pl.BlockSpec((tg, HEAD_DIM), lambda r: (r, gcol)), smem, smem],
        out_specs=pl.BlockSpec((tg, Dg), lambda r: (r, 0)),
        out_shape=jax.ShapeDtypeStruct((T, Dg), F32),
        compiler_params=_cp(("parallel",)),
        name="dn_gates",
    )(P, a_log, dt_bias)

    Tb = _pick(N, (512, 256, 128, 64))
    ntb = N // Tb
    fspec = pl.BlockSpec((Tb, HEAD_DIM), lambda b, h, t: (b * ntb + t, h))
    bspec = pl.BlockSpec((Tb, HEAD_DIM), lambda b, h, t: (b * ntb + ntb - 1 - t, h))
    st = pltpu.VMEM((HEAD_DIM, HEAD_DIM), F32)
    of, ob = pl.pallas_call(
        functools.partial(_dn_kernel, Tb=Tb, C=DN_CHUNK),
        grid=(B, H, ntb),
        in_specs=[fspec] * 4 + [bspec] * 4,
        out_specs=[fspec, bspec],
        out_shape=[jax.ShapeDtypeStruct((T, Dg), F32)] * 2,
        scratch_shapes=[st, st],
        compiler_params=_cp(("parallel", "parallel", "arbitrary")),
        name="delta_net",
    )(qn, kn, vn, gates, qn, kn, vn, gates)

    tr = _pick(T, (512, 256, 128))
    rows = pl.BlockSpec((tr, Dg), lambda r: (r, 0))
    return pl.pallas_call(
        functools.partial(_dn_out_kernel, H=H),
        grid=(T // tr,),
        in_specs=[rows, rows, pl.BlockSpec((tr, Dg), lambda r: (r, 10)), pl.BlockSpec((1, HEAD_DIM), lambda r: (0, 0))],
        out_specs=rows,
        out_shape=jax.ShapeDtypeStruct((T, Dg), BF16),
        compiler_params=_cp(("parallel",)),
        name="dn_out",
    )(of, ob, P, out_g.reshape(1, HEAD_DIM))


def _rope_tables(n):
    pairs = HEAD_DIM // 4
    pos = jnp.arange(n)
    row = (pos // GRID_W).astype(F32)
    col = (pos % GRID_W).astype(F32)
    inv = ROPE_THETA ** (-jnp.arange(pairs, dtype=F32) / pairs)
    ang_r, ang_c = row[:, None] * inv, col[:, None] * inv
    cos = jnp.concatenate([jnp.cos(ang_r)] * 2 + [jnp.cos(ang_c)] * 2, axis=1)
    sin = jnp.concatenate([jnp.sin(ang_r)] * 2 + [jnp.sin(ang_c)] * 2, axis=1)
    first = (jnp.arange(HEAD_DIM) % (2 * pairs)) < pairs
    return cos, jnp.where(first, -sin, 0.0), jnp.where(first, 0.0, sin)


def _permute_w_in(w, Dg, Dkv):
    o_ak = 5 * Dg
    o_lx = o_ak + 2 * Dkv
    o_dq = o_lx + 2 * Dg
    cols = [w[:, :o_ak], w[:, o_lx:o_dq + 4 * Dg], w[:, o_ak:o_lx], w[:, o_dq + 4 * Dg:]]
    w = jnp.concatenate(cols, axis=1)
    n = w.shape[1]
    tile = 1024 if n >= 4096 else 128
    pad = (-n) % tile
    return jnp.pad(w, ((0, 0), (0, pad))).astype(BF16)


def kernel(x_prompt, x_sample, c_prompt, c_sample, ada_w, ada_b, norm_g, w_in, ret_decay, ret_out_g, attn_qk_g,
           attn_out_g, lru_conv_w, lru_conv_b, lru_gate_w, lru_gate_b, lru_lambda, lru_out_g, dn_conv_w, dn_a_log,
           dn_dt_bias, dn_out_g, w_out, ffn_w_gate, ffn_w_up, ffn_conv_w, ffn_conv_b, ffn_w_down):
    b1, N, D = x_prompt.shape
    B = b1 + x_sample.shape[0]
    L = ada_w.shape[0]
    Dg = D // 4
    H = Dg // HEAD_DIM
    Dkv = max(H // 4, 1) * HEAD_DIM
    assert x_sample.shape[1] == N and N % GRID_W == 0 and Dg % HEAD_DIM == 0 and B <= 8

    x = jnp.concatenate([x_prompt.reshape(b1 * N, D), x_sample.reshape(-1, D)], axis=0)
    c8 = jnp.zeros((8, D), F32).at[:B].set(jnp.concatenate([c_prompt, c_sample], axis=0))
    mods = _ada(c8, ada_w, ada_b)
    tabs = _rope_tables(N)

    f = None
    for l in range(L):
        mod = mods[l]
        if l == 0:
            h, = _resnorm(x, B, N, gb=norm_g[l, 0], scale=(mod, 1), shift=(mod, 0))
        else:
            x, h = _resnorm(x, B, N, f=f, ga=norm_g[l - 1, 3], gate=(mods[l - 1], 5),
                            gb=norm_g[l, 0], scale=(mod, 1), shift=(mod, 0))
        P = _mm(h, _permute_w_in(w_in[l], Dg, Dkv), F32, "mm_in")
        ret = _retention(P, tabs, ret_decay[l], ret_out_g[l], B, N, Dg)
        att = _attention(P, tabs, attn_qk_g[l], attn_out_g[l], B, N, Dg)
        rec = _rg_lru(P, lru_conv_w[l], lru_conv_b[l], lru_gate_w[l], lru_gate_b[l], lru_lambda[l], lru_out_g[l], B, N, Dg)
        dn = _delta_net(P, dn_conv_w[l], dn_a_log[l], dn_dt_bias[l], dn_out_g[l], B, N, Dg)
        mixed = _mm_mixed([ret, att, rec, dn], w_out[l].astype(BF16))
        x, h = _resnorm(x, B, N, f=mixed, ga=norm_g[l, 1], gate=(mod, 2), gb=norm_g[l, 2], scale=(mod, 4), shift=(mod, 3))
        a = _ffn_gu(h, ffn_w_gate[l].astype(BF16), ffn_w_up[l].astype(BF16), ffn_conv_w[l], ffn_conv_b[l], N)
        f = _mm(a, ffn_w_down[l].astype(BF16), F32, "mm_down")
    x, = _resnorm(x, B, N, f=f, ga=norm_g[L - 1, 3], gate=(mods[L - 1], 5))
    return x[:b1 * N].reshape(b1, N, D), x[b1 * N:].reshape(B - b1, N, D)
```
